```python
import math
import jax
import jax.numpy as jnp
from jax import lax
import numpy as np

D_MODEL = 4096
BATCH = 2
SEQ = 8192
DEPTH = 4

POOL_WINDOWS = (2, 4, 8, 16)
POOL_GROUPS = 4
POOL_WIDTH = 3 * D_MODEL // 8
POOL_GROUP_DIM = POOL_WIDTH // POOL_GROUPS
SSD_HEAD_DIM = 64
SSD_INNER = D_MODEL // 2
SSD_HEADS = SSD_INNER // SSD_HEAD_DIM
SSD_GROUPS = 4
SSD_STATE = 128
SSD_CONV = 5
SSD_CHUNK = 128
SSD_BC = SSD_GROUPS * SSD_STATE
SSD_XBC = SSD_INNER + 2 * SSD_BC
DILATED_PATTERNS = ((128, 1), (512, 4), (2048, 16))
ATTN_GROUPS = 3
ATTN_HEADS = D_MODEL // 1024
ATTN_HEAD_DIM = 128
ATTN_OUT = ATTN_HEADS * ATTN_HEAD_DIM
ATTN_QKV = 3 * ATTN_GROUPS * ATTN_OUT
T5_BUCKETS = 32
T5_MAX_DISTANCE = 1024
N_BRANCHES = 3
GATE_RANK = D_MODEL // 8
MLP_HIDDEN = 2 * D_MODEL
MLP_CONV = 3
N_MOD = 6
EPS = 1e-6
NEG_INF = -1e30
IN_SPLITS = (POOL_WIDTH, SSD_INNER, SSD_XBC, 2 * SSD_HEADS, ATTN_QKV, GATE_RANK)
N_IN = POOL_WIDTH + SSD_INNER + SSD_XBC + 2 * SSD_HEADS + ATTN_QKV + GATE_RANK

kernel_name = 'hybrid_pool_ssd_dilated_attn_encoder'


def _rmsnorm(t, g):
    t32 = t.astype(jnp.float32)
    t32 = t32 * lax.rsqrt(jnp.mean(t32 * t32, axis=-1, keepdims=True) + EPS)
    return t32.astype(t.dtype) * g


def _dwconv(u, w, bias):
    width = w.shape[0]
    out = lax.conv_general_dilated(u, w[:, None, :], window_strides=(1,), padding=[(width // 2, width // 2)],
                                   dimension_numbers=('NWC', 'WIO', 'NWC'), feature_group_count=u.shape[-1])
    return out + bias


def _pool_mixer(u, pool_w, pool_scale):
    b, l, _ = u.shape
    u32 = u.astype(jnp.float32)
    csum = jnp.concatenate([jnp.zeros((b, 1, POOL_WIDTH), jnp.float32), jnp.cumsum(u32, axis=1)], axis=1)
    pos = np.arange(l)
    groups = []
    for gi, win in enumerate(POOL_WINDOWS):
        lo = np.clip(pos - win // 2, 0, l)
        hi = np.clip(pos + win - win // 2, 0, l)
        sl = slice(gi * POOL_GROUP_DIM, (gi + 1) * POOL_GROUP_DIM)
        cnt = jnp.asarray((hi - lo).astype(np.float32))[None, :, None]
        mean = (csum[:, hi, sl] - csum[:, lo, sl]) / cnt
        groups.append(mean - u32[:, :, sl])
    p = jnp.stack(groups, axis=2).astype(u.dtype)
    y = jnp.einsum('blgc,gcd->blgd', p, pool_w).reshape(b, l, POOL_WIDTH)
    return y * pool_scale


def _ssd_scan(xh, dt, a, bm, cm):
    b, l, nh, hp = xh.shape
    ng, ns = bm.shape[2], bm.shape[3]
    r = nh // ng
    tc = SSD_CHUNK
    nc = l // tc
    xdt = (xh.astype(jnp.float32) * dt[..., None]).reshape(b, nc, tc, ng, r, hp)
    a_cs = jnp.cumsum((dt * a).reshape(b, nc, tc, ng, r).transpose(0, 3, 4, 1, 2), axis=-1)
    bc = bm.astype(jnp.float32).reshape(b, nc, tc, ng, ns)
    cc = cm.astype(jnp.float32).reshape(b, nc, tc, ng, ns)
    lower = np.tril(np.ones((tc, tc), dtype=bool))
    decay = jnp.exp(jnp.where(lower, a_cs[..., :, None] - a_cs[..., None, :], -jnp.inf))
    cb = jnp.einsum('bctgn,bcsgn->bgcts', cc, bc)
    y_diag = jnp.einsum('bgrcts,bcsgrp->bctgrp', cb[:, :, None] * decay, xdt)
    to_end = jnp.exp(a_cs[..., -1:] - a_cs).transpose(0, 3, 4, 1, 2)
    chunk_states = jnp.einsum('bcsgn,bcsgrp->cbgrpn', bc, xdt * to_end[..., None])
    chunk_decay = jnp.exp(a_cs[..., -1]).transpose(3, 0, 1, 2)

    def step(state, inp):
        st, dec = inp
        return state * dec[..., None, None] + st, state

    init = jnp.zeros((b, ng, r, hp, ns), jnp.float32)
    _, prev = lax.scan(step, init, (chunk_states, chunk_decay))
    from_start = jnp.exp(a_cs).transpose(0, 3, 4, 1, 2)
    y_off = jnp.einsum('bctgn,cbgrpn->bctgrp', cc, prev) * from_start[..., None]
    return (y_diag + y_off).reshape(b, l, nh, hp)


def _ssd_mixer(z, xbc, dt_raw, conv_w, conv_b, dt_bias, a_log, d_skip, norm_g):
    b, l, _ = z.shape
    xbc = jax.nn.silu(_dwconv(xbc, conv_w, conv_b))
    xs, bm, cm = jnp.split(xbc, [SSD_INNER, SSD_INNER + SSD_BC], axis=-1)
    xh = xs.reshape(b, l, SSD_HEADS, SSD_HEAD_DIM)
    bm = bm.reshape(b, l, SSD_GROUPS, SSD_STATE)
    cm = cm.reshape(b, l, SSD_GROUPS, SSD_STATE)
    dt = jax.nn.softplus(dt_raw.astype(jnp.float32).reshape(b, l, 2, SSD_HEADS) + dt_bias.astype(jnp.float32))
    a = -jnp.exp(a_log.astype(jnp.float32))
    y_fwd = _ssd_scan(xh, dt[:, :, 0], a[0], bm, cm)
    flip = lambda t: jnp.flip(t, axis=1)
    y_bwd = flip(_ssd_scan(flip(xh), flip(dt[:, :, 1]), a[1], flip(bm), flip(cm)))
    y = y_fwd + y_bwd + xh.astype(jnp.float32) * d_skip.astype(jnp.float32)[:, None]
    y = y.reshape(b, l, SSD_INNER) * jax.nn.silu(z.astype(jnp.float32))
    yg = y.reshape(b, l, SSD_GROUPS, SSD_INNER // SSD_GROUPS)
    yg = yg * lax.rsqrt(jnp.mean(yg * yg, axis=-1, keepdims=True) + EPS)
    return yg.reshape(b, l, SSD_INNER).astype(z.dtype) * norm_g


def _t5_buckets(rel):
    half = T5_BUCKETS // 2
    max_exact = half // 2
    n = np.abs(rel)
    large = max_exact + (np.log(np.maximum(n, max_exact) / max_exact) / np.log(T5_MAX_DISTANCE / max_exact)
                         * (half - max_exact)).astype(np.int32)
    large = np.minimum(large, half - 1)
    return (rel > 0).astype(np.int32) * half + np.where(n < max_exact, n, large).astype(np.int32)


def _dilated_window_attention(q, k, v, bias_table, window, dil):
    b, l, h, e = q.shape
    radius = window // (2 * dil)
    blk = radius
    m = l // dil
    nb = -(-m // blk)
    pad = nb * blk - m

    def sub(t):
        return t.reshape(b, m, dil, h, e).transpose(0, 2, 1, 3, 4)

    qs = jnp.pad(sub(q), ((0, 0), (0, 0), (0, pad), (0, 0), (0, 0))).reshape(b, dil, nb, blk, h, e)

    def key_windows(t):
        tp = jnp.pad(sub(t), ((0, 0), (0, 0), (blk, pad + blk), (0, 0), (0, 0))).reshape(b, dil, nb + 2, blk, h, e)
        return jnp.concatenate([tp[:, :, :-2], tp[:, :, 1:-1], tp[:, :, 2:]], axis=3)

    ks = key_windows(k)
    vs = key_windows(v)
    delta = np.arange(3 * blk)[None, :] - blk - np.arange(blk)[:, None]
    t_key = (np.arange(nb)[:, None] - 1) * blk + np.arange(3 * blk)[None, :]
    mask = (np.abs(delta) <= radius)[None] & ((t_key >= 0) & (t_key < m))[:, None, :]
    bias = jnp.transpose(bias_table[_t5_buckets(delta * dil)], (2, 0, 1)).astype(jnp.float32)
    logits = jnp.einsum('bdnqhe,bdnkhe->bdnhqk', qs, ks).astype(jnp.float32) * (e ** -0.5) + bias
    logits = jnp.where(mask[None, None, :, None], logits, NEG_INF)
    mx = jnp.max(logits, axis=-1, keepdims=True)
    ex = jnp.exp(logits - mx)
    den = jnp.sum(ex, axis=-1, keepdims=True)
    o = jnp.einsum('bdnhqk,bdnkhe->bdnqhe', (ex / den).astype(v.dtype), vs)
    lse = (mx + jnp.log(den))[..., 0]
    o = o.reshape(b, dil, nb * blk, h, e)[:, :, :m].transpose(0, 2, 1, 3, 4).reshape(b, l, h, e)
    lse = lse.transpose(0, 1, 2, 4, 3).reshape(b, dil, nb * blk, h)[:, :, :m].transpose(0, 2, 1, 3).reshape(b, l, h)
    return o, lse


def _dilated_attention_mixer(qkv, q_norm, k_norm, t5_table):
    b, l, _ = qkv.shape
    qkv = qkv.reshape(b, l, 3, ATTN_GROUPS, ATTN_HEADS, ATTN_HEAD_DIM)
    q = _rmsnorm(qkv[:, :, 0], q_norm)
    k = _rmsnorm(qkv[:, :, 1], k_norm)
    v = qkv[:, :, 2]
    outs, lses = [], []
    for gi, (window, dil) in enumerate(DILATED_PATTERNS):
        table = t5_table[:, gi * ATTN_HEADS:(gi + 1) * ATTN_HEADS]
        o, s = _dilated_window_attention(q[:, :, gi], k[:, :, gi], v[:, :, gi], table, window, dil)
        outs.append(o)
        lses.append(s)
    wts = jax.nn.softmax(jnp.stack(lses, axis=0), axis=0)
    out = jnp.einsum('gblh,gblhe->blhe', wts, jnp.stack(outs, axis=0).astype(jnp.float32))
    return out.reshape(b, l, ATTN_OUT).astype(qkv.dtype)


def _mixer_block(h, w_in, pool_w, pool_scale, ssd_conv_w, ssd_conv_b, ssd_dt_bias, ssd_a_log, ssd_d, ssd_norm,
                 q_norm, k_norm, t5_table, proj_a, proj_b, proj_c, gate_up, gate_b, w_out):
    b, l, _ = h.shape
    offsets = [int(o) for o in np.cumsum(IN_SPLITS)[:-1]]
    a_in, z, xbc, dt_raw, qkv, g_low = jnp.split(h @ w_in, offsets, axis=-1)
    y_a = _pool_mixer(a_in, pool_w, pool_scale)
    y_b = _ssd_mixer(z, xbc, dt_raw, ssd_conv_w, ssd_conv_b, ssd_dt_bias, ssd_a_log, ssd_d, ssd_norm)
    y_c = _dilated_attention_mixer(qkv, q_norm, k_norm, t5_table)
    gates = jax.nn.sigmoid(g_low @ gate_up + gate_b).reshape(b, l, N_BRANCHES, D_MODEL)
    merged = gates[:, :, 0] * (y_a @ proj_a) + gates[:, :, 1] * (y_b @ proj_b) + gates[:, :, 2] * (y_c @ proj_c)
    return merged @ w_out


def _conv_ffn(h, w_up, conv_w, conv_b, w_down):
    u, v = jnp.split(_dwconv(h @ w_up, conv_w, conv_b), 2, axis=-1)
    return (jax.nn.silu(u) * v) @ w_down


def setup_inputs(seed: int = 0) -> dict:
    key = jax.random.key(seed)
    ks = jax.random.split(key, 32)
    f32 = jnp.float32

    def normal(k, shape, scale):
        return jax.random.normal(k, shape, f32) * scale

    def gain(k, shape):
        return 1.0 + normal(k, shape, 0.02)

    dt_init = jnp.exp(jax.random.uniform(ks[13], (DEPTH, 2, SSD_HEADS), f32, math.log(1e-3), math.log(1e-1)))
    return {
        'x': normal(ks[0], (BATCH, SEQ, D_MODEL), 1.0),
        'c': normal(ks[1], (BATCH, D_MODEL), 1.0),
        'ada_w': normal(ks[2], (D_MODEL, N_MOD * D_MODEL), 0.5 * D_MODEL ** -0.5),
        'ada_b': normal(ks[3], (N_MOD * D_MODEL,), 0.01),
        'ada_layer': normal(ks[4], (DEPTH, N_MOD * D_MODEL), 0.1),
        't5_table': normal(ks[5], (T5_BUCKETS, ATTN_GROUPS * ATTN_HEADS), 0.5),
        'norm_mix': gain(ks[6], (DEPTH, D_MODEL)),
        'norm_mlp': gain(ks[7], (DEPTH, D_MODEL)),
        'w_in': normal(ks[8], (DEPTH, D_MODEL, N_IN), D_MODEL ** -0.5),
        'pool_w': normal(ks[9], (DEPTH, POOL_GROUPS, POOL_GROUP_DIM, POOL_GROUP_DIM), POOL_GROUP_DIM ** -0.5),
        'pool_scale': gain(ks[10], (DEPTH, POOL_WIDTH)),
        'ssd_conv_w': normal(ks[11], (DEPTH, SSD_CONV, SSD_XBC), SSD_CONV ** -0.5),
        'ssd_conv_b': normal(ks[12], (DEPTH, SSD_XBC), 0.01),
        'ssd_dt_bias': dt_init + jnp.log(-jnp.expm1(-dt_init)),
        'ssd_a_log': jnp.log(jax.random.uniform(ks[14], (DEPTH, 2, SSD_HEADS), f32, 1.0, 16.0)),
        'ssd_d': 1.0 + normal(ks[15], (DEPTH, SSD_HEADS), 0.1),
        'ssd_norm': gain(ks[16], (DEPTH, SSD_INNER)),
        'q_norm': gain(ks[17], (DEPTH, ATTN_HEAD_DIM)),
        'k_norm': gain(ks[18], (DEPTH, ATTN_HEAD_DIM)),
        'proj_a': normal(ks[19], (DEPTH, POOL_WIDTH, D_MODEL), POOL_WIDTH ** -0.5),
        'proj_b': normal(ks[20], (DEPTH, SSD_INNER, D_MODEL), SSD_INNER ** -0.5),
        'proj_c': normal(ks[21], (DEPTH, ATTN_OUT, D_MODEL), ATTN_OUT ** -0.5),
        'gate_up': normal(ks[22], (DEPTH, GATE_RANK, N_BRANCHES * D_MODEL), GATE_RANK ** -0.5),
        'gate_b': normal(ks[23], (DEPTH, N_BRANCHES * D_MODEL), 0.01),
        'w_out': normal(ks[24], (DEPTH, D_MODEL, D_MODEL), D_MODEL ** -0.5),
        'mlp_up': normal(ks[25], (DEPTH, D_MODEL, 2 * MLP_HIDDEN), D_MODEL ** -0.5),
        'mlp_conv_w': normal(ks[26], (DEPTH, MLP_CONV, 2 * MLP_HIDDEN), MLP_CONV ** -0.5),
        'mlp_conv_b': normal(ks[27], (DEPTH, 2 * MLP_HIDDEN), 0.01),
        'mlp_down': normal(ks[28], (DEPTH, MLP_HIDDEN, D_MODEL), MLP_HIDDEN ** -0.5),
    }


def reference(x, c, ada_w, ada_b, ada_layer, t5_table, norm_mix, norm_mlp, w_in, pool_w, pool_scale,
              ssd_conv_w, ssd_conv_b, ssd_dt_bias, ssd_a_log, ssd_d, ssd_norm, q_norm, k_norm,
              proj_a, proj_b, proj_c, gate_up, gate_b, w_out, mlp_up, mlp_conv_w, mlp_conv_b, mlp_down):
    b = x.shape[0]
    mod_shared = jax.nn.silu(c) @ ada_w + ada_b
    for layer in range(DEPTH):
        mod = (mod_shared + ada_layer[layer]).reshape(b, N_MOD, 1, D_MODEL)
        shift_m, scale_m, gate_m = mod[:, 0], mod[:, 1], mod[:, 2]
        shift_f, scale_f, gate_f = mod[:, 3], mod[:, 4], mod[:, 5]
        h = _rmsnorm(x, norm_mix[layer]) * (1 + scale_m) + shift_m
        x = x + gate_m * _mixer_block(h, w_in[layer], pool_w[layer], pool_scale[layer], ssd_conv_w[layer],
                                      ssd_conv_b[layer], ssd_dt_bias[layer], ssd_a_log[layer], ssd_d[layer],
                                      ssd_norm[layer], q_norm[layer], k_norm[layer], t5_table, proj_a[layer],
                                      proj_b[layer], proj_c[layer], gate_up[layer], gate_b[layer], w_out[layer])
        h = _rmsnorm(x, norm_mlp[layer]) * (1 + scale_f) + shift_f
        x = x + gate_f * _conv_ffn(h, mlp_up[layer], mlp_conv_w[layer], mlp_conv_b[layer], mlp_down[layer])
    return x
```

```python
import functools

import numpy as np
import jax
import jax.numpy as jnp
from jax import lax
from jax.experimental import pallas as pl
from jax.experimental.pallas import tpu as pltpu

F32 = jnp.float32
BF16 = jnp.bfloat16

D_MODEL = 4096
DEPTH = 4
N_MOD = 6
EPS = 1e-6
NEG_INF = -1e30

POOL_WINDOWS = (2, 4, 8, 16)
POOL_WIDTH = 1536
POOL_GROUP_DIM = 384

SSD_INNER = 2048
SSD_HEADS = 32
SSD_HEAD_DIM = 64
SSD_GROUPS = 4
SSD_STATE = 128
SSD_CHUNK = 128
SSD_BC = 512
SSD_XBC = 3072
SSD_GROUP_WIDTH = SSD_INNER // SSD_GROUPS

DILATIONS = (1, 4, 16)
ATTN_RADIUS = 64
ATTN_GROUPS = 3
ATTN_HEADS = 4
ATTN_HEAD_DIM = 128
ATTN_OUT = 512
ATTN_QKV = 4608
T5_BUCKETS = 32
T5_MAX_DISTANCE = 1024
GATE_RANK = 512
MLP_HIDDEN = 8192

HALO = 16
LANE = 128
MIB = 1024 * 1024


def _params(sem, vmem_mib):
    return pltpu.CompilerParams(dimension_semantics=sem, vmem_limit_bytes=vmem_mib * MIB)


def _sigmoid(x):
    return 1.0 / (1.0 + jnp.exp(-x))


def _silu(x):
    return x * _sigmoid(x)


def _softplus(x):
    return jnp.maximum(x, 0.0) + jnp.log1p(jnp.exp(-jnp.abs(x)))


def _mod_kernel(c_ref, w_ref, b_ref, al_ref, o_ref):
    s = _silu(c_ref[...])
    acc = jnp.dot(s.astype(BF16), w_ref[...].astype(BF16), preferred_element_type=F32)
    base = acc + b_ref[...]
    o_ref[...] = base[None, :, :] + al_ref[...][:, None, :]


def _modulation(c, ada_w, ada_b, ada_layer):
    b = c.shape[0]
    n = ada_w.shape[1]
    tn = 512
    c8 = jnp.zeros((8, D_MODEL), F32).at[:b].set(c)
    out = pl.pallas_call(
        _mod_kernel,
        grid=(n // tn,),
        in_specs=[
            pl.BlockSpec((8, D_MODEL), lambda j: (0, 0)),
            pl.BlockSpec((D_MODEL, tn), lambda j: (0, j)),
            pl.BlockSpec((1, tn), lambda j: (0, j)),
            pl.BlockSpec((DEPTH, tn), lambda j: (0, j)),
        ],
        out_specs=pl.BlockSpec((DEPTH, 8, tn), lambda j: (0, 0, j)),
        out_shape=jax.ShapeDtypeStruct((DEPTH, 8, n), F32),
        compiler_params=_params(("arbitrary",), 40),
        name="adaln_mod",
    )(c8, ada_w, ada_b.reshape(1, n), ada_layer)
    return out[:, :b].reshape(DEPTH, b, N_MOD, 1, D_MODEL)


def _norm_kernel(x_ref, g_ref, sc_ref, sh_ref, o_ref):
    x = x_ref[...]
    ms = jnp.mean(x * x, axis=-1, keepdims=True)
    y = x * lax.rsqrt(ms + EPS) * g_ref[...]
    o_ref[...] = (y * (1.0 + sc_ref[...]) + sh_ref[...]).astype(o_ref.dtype)


def _norm_mod(x, g, scale, shift):
    b, l, d = x.shape
    ts = 512
    return pl.pallas_call(
        _norm_kernel,
        grid=(b, l // ts),
        in_specs=[
            pl.BlockSpec((None, ts, d), lambda bi, i: (bi, i, 0)),
            pl.BlockSpec((1, d), lambda bi, i: (0, 0)),
            pl.BlockSpec((None, 1, d), lambda bi, i: (bi, 0, 0)),
            pl.BlockSpec((None, 1, d), lambda bi, i: (bi, 0, 0)),
        ],
        out_specs=pl.BlockSpec((None, ts, d), lambda bi, i: (bi, i, 0)),
        out_shape=jax.ShapeDtypeStruct((b, l, d), BF16),
        compiler_params=_params(("parallel", "parallel"), 40),
        name="norm_mod",
    )(x, g.reshape(1, d), scale, shift)


def _mm_kernel(a_ref, b_ref, o_ref):
    o_ref[...] = jnp.dot(a_ref[...], b_ref[...], preferred_element_type=F32).astype(o_ref.dtype)


def _matmul(a, w, tm, tn, out_dtype, name):
    m, k = a.shape
    n = w.shape[1]
    return pl.pallas_call(
        _mm_kernel,
        grid=(m // tm, n // tn),
        in_specs=[
            pl.BlockSpec((tm, k), lambda i, j: (i, 0)),
            pl.BlockSpec((k, tn), lambda i, j: (0, j)),
        ],
        out_specs=pl.BlockSpec((tm, tn), lambda i, j: (i, j)),
        out_shape=jax.ShapeDtypeStruct((m, n), out_dtype),
        compiler_params=_params(("parallel", "arbitrary"), 56),
        name=name,
    )(a, w)


def _mm_res_kernel(a_ref, b_ref, x_ref, g_ref, o_ref):
    acc = jnp.dot(a_ref[...], b_ref[...], preferred_element_type=F32)
    o_ref[...] = x_ref[...] + g_ref[...] * acc


def _matmul_residual(a, w, x, gate, tm, tn, name):
    m, k = a.shape
    n = w.shape[1]
    blocks_per_batch = (m // gate.shape[0]) // tm
    return pl.pallas_call(
        _mm_res_kernel,
        grid=(m // tm, n // tn),
        in_specs=[
            pl.BlockSpec((tm, k), lambda i, j: (i, 0)),
            pl.BlockSpec((k, tn), lambda i, j: (0, j)),
            pl.BlockSpec((tm, tn), lambda i, j: (i, j)),
            pl.BlockSpec((None, 1, tn), lambda i, j: (i // blocks_per_batch, 0, j)),
        ],
        out_specs=pl.BlockSpec((tm, tn), lambda i, j: (i, j)),
        out_shape=jax.ShapeDtypeStruct((m, n), F32),
        compiler_params=_params(("parallel", "arbitrary"), 56),
        name=name,
    )(a, w, x, gate)


_WIN_TN = 512
_WIN_SEGMENTS = (POOL_WIDTH, SSD_INNER, SSD_XBC, ATTN_QKV, GATE_RANK)
_WIN_TILE_START = tuple(int(s) // _WIN_TN for s in np.cumsum((0,) + _WIN_SEGMENTS))
_WIN_TILES = _WIN_TILE_START[-1] + 1
DT_LANES = LANE


def _win_kernel(a_ref, b_ref, *o_refs):
    j = pl.program_id(1)
    acc = jnp.dot(a_ref[...], b_ref[...], preferred_element_type=F32)
    for s, o_ref in enumerate(o_refs[:-1]):
        @pl.when((j >= _WIN_TILE_START[s]) & (j < _WIN_TILE_START[s + 1]))
        def _(o_ref=o_ref):
            o_ref[...] = acc.astype(o_ref.dtype)

    @pl.when(j == _WIN_TILES - 1)
    def _():
        o_refs[-1][...] = acc[:, :DT_LANES]


def _in_projection(h, w_in_r, tm):
    m, k = h.shape

    def seg_spec(s):
        lo, n_tiles = _WIN_TILE_START[s], _WIN_TILE_START[s + 1] - _WIN_TILE_START[s]
        return pl.BlockSpec((tm, _WIN_TN), lambda i, j: (i, jnp.clip(j - lo, 0, n_tiles - 1)))

    out_specs = [seg_spec(s) for s in range(len(_WIN_SEGMENTS))]
    out_specs.append(pl.BlockSpec((tm, DT_LANES), lambda i, j: (i, 0)))
    out_shape = [jax.ShapeDtypeStruct((m, w), BF16) for w in _WIN_SEGMENTS]
    out_shape.append(jax.ShapeDtypeStruct((m, DT_LANES), F32))
    return pl.pallas_call(
        _win_kernel,
        grid=(m // tm, _WIN_TILES),
        in_specs=[
            pl.BlockSpec((tm, k), lambda i, j: (i, 0)),
            pl.BlockSpec((k, _WIN_TN), lambda i, j: (0, j)),
        ],
        out_specs=out_specs,
        out_shape=out_shape,
        compiler_params=_params(("parallel", "arbitrary"), 56),
        name="in_proj",
    )(h, w_in_r)


def _halo_specs(ts, width, seq_len, col):
    r = ts // HALO
    last = seq_len // HALO - 1
    return [
        pl.BlockSpec((None, ts, width), lambda b, i, j: (b, i, col(j))),
        pl.BlockSpec((None, HALO, width), lambda b, i, j: (b, jnp.maximum(i * r - 1, 0), col(j))),
        pl.BlockSpec((None, HALO, width), lambda b, i, j: (b, jnp.minimum((i + 1) * r, last), col(j))),
    ]


def _extended(main_ref, prev_ref, next_ref, i, n_tiles):
    prev = jnp.where(i > 0, prev_ref[...].astype(F32), 0.0)
    nxt = jnp.where(i < n_tiles - 1, next_ref[...].astype(F32), 0.0)
    return jnp.concatenate([prev, main_ref[...].astype(F32), nxt], axis=0)


def _shift_rows(ext, k):
    if k == 0:
        return ext
    n = ext.shape[0]
    return pltpu.roll(ext, (-k) % n, axis=0)


def _centre(ext, ts):
    return ext[HALO:HALO + ts]


def _pool_kernel(m_ref, p_ref, n_ref, w_ref, sc_ref, o_ref, *, ts, seq_len):
    i = pl.program_id(1)
    ext = _extended(m_ref, p_ref, n_ref, i, seq_len // ts)
    pos = i * ts + lax.broadcasted_iota(jnp.int32, (ts, 1), 0)
    for gi, win in enumerate(POOL_WINDOWS):
        sl = slice(gi * POOL_GROUP_DIM, (gi + 1) * POOL_GROUP_DIM)
        e = ext[:, sl]
        acc = _shift_rows(e, -1) + e
        w = 2
        while w < win:
            acc = _shift_rows(acc, -(w // 2)) + _shift_rows(acc, w // 2)
            w *= 2
        lo = jnp.maximum(pos - win // 2, 0)
        hi = jnp.minimum(pos + win - win // 2, seq_len)
        cnt = (hi - lo).astype(F32)
        p = (_centre(acc, ts) / cnt - _centre(e, ts)).astype(BF16)
        y = jnp.dot(p, w_ref[gi], preferred_element_type=F32)
        o_ref[:, sl] = (y * sc_ref[:, sl]).astype(o_ref.dtype)


def _pool_mixer(a_in, pool_w, pool_scale):
    b, l, c = a_in.shape
    ts = 512
    kern = functools.partial(_pool_kernel, ts=ts, seq_len=l)
    return pl.pallas_call(
        kern,
        grid=(b, l // ts, 1),
        in_specs=_halo_specs(ts, c, l, lambda j: 0) + [
            pl.BlockSpec((len(POOL_WINDOWS), POOL_GROUP_DIM, POOL_GROUP_DIM), lambda bi, i, j: (0, 0, 0)),
            pl.BlockSpec((1, c), lambda bi, i, j: (0, 0)),
        ],
        out_specs=pl.BlockSpec((None, ts, c), lambda bi, i, j: (bi, i, 0)),
        out_shape=jax.ShapeDtypeStruct((b, l, c), BF16),
        compiler_params=_params(("parallel", "parallel", "arbitrary"), 48),
        name="pool_mixer",
    )(a_in, a_in, a_in, pool_w, pool_scale.reshape(1, c))


def _dwconv(ext, w_ref, b_ref, ts):
    width = w_ref.shape[0]
    acc = None
    for k in range(width):
        term = _centre(_shift_rows(ext, k - width // 2), ts) * w_ref[k:k + 1, :]
        acc = term if acc is None else acc + term
    return acc + b_ref[...]


def _ssd_conv_kernel(m_ref, p_ref, n_ref, w_ref, b_ref, o_ref, *, ts, seq_len):
    ext = _extended(m_ref, p_ref, n_ref, pl.program_id(1), seq_len // ts)
    o_ref[...] = _silu(_dwconv(ext, w_ref, b_ref, ts)).astype(o_ref.dtype)


def _ssd_conv(xbc, conv_w, conv_b):
    b, l, c = xbc.shape
    ts, tc = 512, 512
    kern = functools.partial(_ssd_conv_kernel, ts=ts, seq_len=l)
    return pl.pallas_call(
        kern,
        grid=(b, l // ts, c // tc),
        in_specs=_halo_specs(ts, tc, l, lambda j: j) + [
            pl.BlockSpec((conv_w.shape[0], tc), lambda bi, i, j: (0, j)),
            pl.BlockSpec((1, tc), lambda bi, i, j: (0, j)),
        ],
        out_specs=pl.BlockSpec((None, ts, tc), lambda bi, i, j: (bi, i, j)),
        out_shape=jax.ShapeDtypeStruct((b, l, c), BF16),
        compiler_params=_params(("parallel", "parallel", "arbitrary"), 32),
        name="ssd_conv",
    )(xbc, xbc, xbc, conv_w, conv_b.reshape(1, c))


def _ffn_act_kernel(um_ref, up_ref, un_ref, vm_ref, vp_ref, vn_ref, wu_ref, bu_ref, wv_ref, bv_ref, o_ref,
                    *, ts, seq_len):
    i = pl.program_id(1)
    n_tiles = seq_len // ts
    u = _dwconv(_extended(um_ref, up_ref, un_ref, i, n_tiles), wu_ref, bu_ref, ts)
    v = _dwconv(_extended(vm_ref, vp_ref, vn_ref, i, n_tiles), wv_ref, bv_ref, ts)
    o_ref[...] = (_silu(u) * v).astype(o_ref.dtype)


def _ffn_act(up, conv_w, conv_b):
    b, l, c2 = up.shape
    hid = c2 // 2
    ts, tc = 512, 512
    off = hid // tc
    kern = functools.partial(_ffn_act_kernel, ts=ts, seq_len=l)
    wspec = lambda o: [pl.BlockSpec((conv_w.shape[0], tc), lambda bi, i, j: (0, j + o)),
                       pl.BlockSpec((1, tc), lambda bi, i, j: (0, j + o))]
    conv_b2 = conv_b.reshape(1, c2)
    return pl.pallas_call(
        kern,
        grid=(b, l // ts, hid // tc),
        in_specs=(_halo_specs(ts, tc, l, lambda j: j) + _halo_specs(ts, tc, l, lambda j: j + off)
                  + wspec(0) + wspec(off)),
        out_specs=pl.BlockSpec((None, ts, tc), lambda bi, i, j: (bi, i, j)),
        out_shape=jax.ShapeDtypeStruct((b, l, hid), BF16),
        compiler_params=_params(("parallel", "parallel", "arbitrary"), 32),
        name="ffn_conv_act",
    )(up, up, up, up, up, up, conv_w, conv_b2, conv_w, conv_b2)


def _split_dot(lhs, rhs, terms, split_lhs):
    x = lhs if split_lhs else rhs
    acc = None
    for _ in range(terms):
        hi = x.astype(BF16)
        part = (jnp.dot(hi, rhs, preferred_element_type=F32) if split_lhs
                else jnp.dot(lhs, hi, preferred_element_type=F32))
        acc = part if acc is None else acc + part
        x = x - hi.astype(F32)
    return acc


def _ssd_kernel(xs_ref, bm_ref, cm_ref, dtr_ref, dtb_ref, alog_ref, e_ref, *rest, reverse, final):
    if final:
        yf_ref, z_ref, dsk_ref, ng_ref, o_ref, st_ref = rest
    else:
        o_ref, st_ref = rest
    tc = SSD_CHUNK
    gw = SSD_GROUP_WIDTH

    @pl.when(pl.program_id(1) == 0)
    def _():
        st_ref[...] = jnp.zeros_like(st_ref)

    lane0 = SSD_HEADS if reverse else 0
    dt = _softplus(dtr_ref[...] + dtb_ref[...])
    dta = dt * (-jnp.exp(alog_ref[...]))
    ti = lax.broadcasted_iota(jnp.int32, (tc, tc), 0)
    ui = lax.broadcasted_iota(jnp.int32, (tc, tc), 1)
    causal = (ui >= ti) if reverse else (ui <= ti)
    cs = _split_dot(jnp.where(causal, 1.0, 0.0).astype(BF16), dta, 3, split_lhs=False)
    total = cs[0:1] if reverse else cs[tc - 1:tc]
    to_end = jnp.exp(total - cs)
    from_start = jnp.exp(cs)
    chunk_decay = jnp.broadcast_to(jnp.exp(total), (8, LANE))
    stacked = jnp.concatenate([dt, dt * to_end, from_start, chunk_decay], axis=0)
    wide = _split_dot(stacked, e_ref[...], 2, split_lhs=True)
    dt_e, w1_e, fs_e, cd_e = wide[0:tc], wide[tc:2 * tc], wide[2 * tc:3 * tc], wide[3 * tc:3 * tc + 1]

    xs = xs_ref[...].astype(F32)
    xdt = (xs * dt_e).astype(BF16)
    xw = (xs * w1_e).astype(BF16)
    cs_t = cs.T
    lane = lax.broadcasted_iota(jnp.int32, (tc, LANE), 1)
    first_head = lane < SSD_HEAD_DIM

    for g in range(SSD_GROUPS):
        gs = slice(g * gw, (g + 1) * gw)
        bm = bm_ref[:, g * SSD_STATE:(g + 1) * SSD_STATE]
        cm = cm_ref[:, g * SSD_STATE:(g + 1) * SSD_STATE]
        cb = lax.dot_general(cm, bm, (((1,), (1,)), ((), ())), preferred_element_type=F32)
        bm_t = bm.astype(F32).T.astype(BF16)
        s_prev = st_ref[g]
        y_off = jnp.dot(cm, s_prev.astype(BF16), preferred_element_type=F32) * fs_e[:, gs]
        s_chunk = jnp.dot(bm_t, xw[:, gs], preferred_element_type=F32)
        st_ref[g] = s_prev * cd_e[:, gs] + s_chunk
        parts = []
        for pr in range(gw // LANE):
            h0 = g * (gw // SSD_HEAD_DIM) + 2 * pr
            ms = []
            for h in (h0, h0 + 1):
                col = cs[:, lane0 + h:lane0 + h + 1]
                row = cs_t[lane0 + h:lane0 + h + 1, :]
                decay = jnp.exp(jnp.where(causal, col - row, NEG_INF))
                ms.append((cb * decay).astype(BF16))
            xp = xdt[:, h0 * SSD_HEAD_DIM:(h0 + 2) * SSD_HEAD_DIM]
            zero = jnp.zeros_like(xp)
            rhs = jnp.concatenate([jnp.where(first_head, xp, zero), jnp.where(first_head, zero, xp)], axis=0)
            y_diag = jnp.dot(jnp.concatenate(ms, axis=1), rhs, preferred_element_type=F32)
            parts.append(y_diag + y_off[:, pr * LANE:(pr + 1) * LANE])
        y = jnp.concatenate(parts, axis=1)
        if final:
            y = y + yf_ref[:, gs] + xs[:, gs] * dsk_ref[:, gs]
            y = y * _silu(z_ref[:, gs].astype(F32))
            y = y * lax.rsqrt(jnp.mean(y * y, axis=-1, keepdims=True) + EPS)
            o_ref[:, gs] = (y * ng_ref[:, gs]).astype(o_ref.dtype)
        else:
            o_ref[:, gs] = y


def _head_expander(reverse):
    e = np.zeros((LANE, SSD_INNER), np.float32)
    lane0 = SSD_HEADS if reverse else 0
    for h in range(SSD_HEADS):
        e[lane0 + h, h * SSD_HEAD_DIM:(h + 1) * SSD_HEAD_DIM] = 1.0
    return jnp.asarray(e, BF16)


def _ssd_pass(xbc_act, dt_raw, dt_bias_row, a_log_row, reverse, final_args=None):
    b, l, _ = xbc_act.shape
    tc = SSD_CHUNK
    nc = l // tc
    final = final_args is not None
    chunk = (lambda c: nc - 1 - c) if reverse else (lambda c: c)
    seq = lambda width, colblk: pl.BlockSpec((None, tc, width), lambda bi, c: (bi, chunk(c), colblk))
    row = lambda width: pl.BlockSpec((1, width), lambda bi, c: (0, 0))
    in_specs = [
        seq(SSD_INNER, 0),
        seq(SSD_BC, SSD_INNER // SSD_BC),
        seq(SSD_BC, SSD_INNER // SSD_BC + 1),
        seq(DT_LANES, 0),
        row(DT_LANES), row(DT_LANES),
        pl.BlockSpec((LANE, SSD_INNER), lambda bi, c: (0, 0)),
    ]
    args = [xbc_act, xbc_act, xbc_act, dt_raw, dt_bias_row, a_log_row, _head_expander(reverse)]
    if final:
        y_fwd, z, d_skip_row, norm_row = final_args
        in_specs += [seq(SSD_INNER, 0), seq(SSD_INNER, 0), row(SSD_INNER), row(SSD_INNER)]
        args += [y_fwd, z, d_skip_row, norm_row]
    return pl.pallas_call(
        functools.partial(_ssd_kernel, reverse=reverse, final=final),
        grid=(b, nc),
        in_specs=in_specs,
        out_specs=seq(SSD_INNER, 0),
        out_shape=jax.ShapeDtypeStruct((b, l, SSD_INNER), BF16 if final else F32),
        scratch_shapes=[pltpu.VMEM((SSD_GROUPS, SSD_STATE, SSD_GROUP_WIDTH), F32)],
        compiler_params=_params(("parallel", "arbitrary"), 32),
        name="ssd_bwd_final" if final else "ssd_fwd",
    )(*args)


def _ssd_mixer(z, xbc, dt_raw, conv_w, conv_b, dt_bias, a_log, d_skip, norm_g):
    xbc_act = _ssd_conv(xbc, conv_w, conv_b)
    pad = DT_LANES - 2 * SSD_HEADS
    dtb = jnp.pad(dt_bias.reshape(1, 2 * SSD_HEADS), ((0, 0), (0, pad)))
    alog = jnp.pad(a_log.reshape(1, 2 * SSD_HEADS), ((0, 0), (0, pad)))
    d_row = jnp.repeat(d_skip, SSD_HEAD_DIM).reshape(1, SSD_INNER)
    y_fwd = _ssd_pass(xbc_act, dt_raw, dtb, alog, reverse=False)
    return _ssd_pass(xbc_act, dt_raw, dtb, alog, reverse=True,
                     final_args=(y_fwd, z, d_row, norm_g.reshape(1, SSD_INNER)))


ATTN_QB = 128
ATTN_KB = ATTN_QB + 2 * ATTN_RADIUS


def _t5_bucket_table():
    half = T5_BUCKETS // 2
    max_exact = half // 2
    delta = np.arange(ATTN_KB)[None, :] - ATTN_RADIUS - np.arange(ATTN_QB)[:, None]
    out = []
    for dil in DILATIONS:
        rel = delta * dil
        n = np.abs(rel)
        large = max_exact + (np.log(np.maximum(n, max_exact) / max_exact) / np.log(T5_MAX_DISTANCE / max_exact)
                             * (half - max_exact)).astype(np.int32)
        large = np.minimum(large, half - 1)
        out.append((rel > 0).astype(np.int32) * half + np.where(n < max_exact, n, large).astype(np.int32))
    return np.stack(out).astype(np.int32)


def _bias_kernel(tab_ref, bkt_ref, o_ref):
    g = pl.program_id(0)
    bk = bkt_ref[...]
    for h in range(ATTN_HEADS):
        acc = jnp.zeros(bk.shape, F32)
        for bucket in range(T5_BUCKETS):
            acc = jnp.where(bk == bucket, tab_ref[bucket, g * ATTN_HEADS + h], acc)
        o_ref[h] = acc


def _t5_bias(t5_table):
    return pl.pallas_call(
        _bias_kernel,
        grid=(ATTN_GROUPS,),
        in_specs=[
            pl.BlockSpec(memory_space=pltpu.SMEM),
            pl.BlockSpec((None, ATTN_QB, ATTN_KB), lambda g: (g, 0, 0)),
        ],
        out_specs=pl.BlockSpec((None, ATTN_HEADS, ATTN_QB, ATTN_KB), lambda g: (g, 0, 0, 0)),
        out_shape=jax.ShapeDtypeStruct((ATTN_GROUPS, ATTN_HEADS, ATTN_QB, ATTN_KB), F32),
        compiler_params=_params(("arbitrary",), 16),
        name="t5_bias",
    )(t5_table, jnp.asarray(_t5_bucket_table()))


def _head_rmsnorm(x, g):
    out = []
    for h in range(ATTN_HEADS):
        xh = x[:, h * ATTN_HEAD_DIM:(h + 1) * ATTN_HEAD_DIM]
        ms = jnp.mean(xh * xh, axis=-1, keepdims=True)
        out.append(((xh * lax.rsqrt(ms + EPS)) * g).astype(BF16))
    return out


def _attn_kernel(q_ref, kp_ref, ko_ref, kn_ref, vp_ref, vo_ref, vn_ref, qg_ref, kg_ref, bias_ref,
                 o_ref, l_ref, *, m):
    i = pl.program_id(2)
    qb, kb, r = ATTN_QB, ATTN_KB, ATTN_RADIUS
    window = lambda p_ref, o_ref_, n_ref: jnp.concatenate(
        [p_ref[qb - r:, :], o_ref_[...], n_ref[:r, :]], axis=0)
    qh = _head_rmsnorm(q_ref[...].astype(F32), qg_ref[...])
    kh = _head_rmsnorm(window(kp_ref, ko_ref, kn_ref).astype(F32), kg_ref[...])
    v = window(vp_ref, vo_ref, vn_ref)
    qq = lax.broadcasted_iota(jnp.int32, (qb, kb), 0)
    kk = lax.broadcasted_iota(jnp.int32, (qb, kb), 1)
    key_pos = i * qb - r + kk
    valid = (jnp.abs(kk - r - qq) <= r) & (key_pos >= 0) & (key_pos < m)
    scale = ATTN_HEAD_DIM ** -0.5
    for h in range(ATTN_HEADS):
        hs = slice(h * ATTN_HEAD_DIM, (h + 1) * ATTN_HEAD_DIM)
        s = lax.dot_general(qh[h], kh[h], (((1,), (1,)), ((), ())), preferred_element_type=F32)
        s = jnp.where(valid, s * scale + bias_ref[h], NEG_INF)
        mx = jnp.max(s, axis=-1, keepdims=True)
        ex = jnp.exp(s - mx)
        den = jnp.sum(ex, axis=-1, keepdims=True)
        o = jnp.dot(ex.astype(BF16), v[:, hs], preferred_element_type=F32)
        o_ref[:, hs] = o / den
        l_ref[:, hs] = jnp.broadcast_to(mx + jnp.log(den), (qb, ATTN_HEAD_DIM))


def _dilated_attention(qkv, q_norm, k_norm, bias, gi):
    b, l, _ = qkv.shape
    dil = DILATIONS[gi]
    m = l // dil
    qb = ATTN_QB
    nblk = m // qb
    cols = ATTN_QKV // ATTN_OUT
    view = qkv.reshape(b, m, dil * ATTN_QKV)

    def spec(which, shift):
        def index(bi, r, i):
            return (bi, jnp.clip(i + shift, 0, nblk - 1), r * cols + which * ATTN_GROUPS + gi)
        return pl.BlockSpec((None, qb, ATTN_OUT), index)

    out_spec = pl.BlockSpec((None, qb, ATTN_OUT), lambda bi, r, i: (bi, i, r))
    norm_spec = pl.BlockSpec((1, ATTN_HEAD_DIM), lambda bi, r, i: (0, 0))
    o, lse = pl.pallas_call(
        functools.partial(_attn_kernel, m=m),
        grid=(b, dil, nblk),
        in_specs=[spec(0, 0), spec(1, -1), spec(1, 0), spec(1, 1), spec(2, -1), spec(2, 0), spec(2, 1),
                  norm_spec, norm_spec,
                  pl.BlockSpec((None, ATTN_HEADS, ATTN_QB, ATTN_KB), lambda bi, r, i: (gi, 0, 0, 0))],
        out_specs=[out_spec, out_spec],
        out_shape=[jax.ShapeDtypeStruct((b, m, dil * ATTN_OUT), F32)] * 2,
        compiler_params=_params(("parallel", "parallel", "arbitrary"), 32),
        name=f"dilated_attn_{dil}",
    )(view, view, view, view, view, view, view,
      q_norm.reshape(1, ATTN_HEAD_DIM), k_norm.reshape(1, ATTN_HEAD_DIM), bias)
    return o.reshape(b, l, ATTN_OUT), lse.reshape(b, l, ATTN_OUT)


def _attn_merge_kernel(o0, o1, o2, l0, l1, l2, out_ref):
    a, b_, c = l0[...], l1[...], l2[...]
    mx = jnp.maximum(jnp.maximum(a, b_), c)
    e0, e1, e2 = jnp.exp(a - mx), jnp.exp(b_ - mx), jnp.exp(c - mx)
    num = e0 * o0[...] + e1 * o1[...] + e2 * o2[...]
    out_ref[...] = (num / (e0 + e1 + e2)).astype(out_ref.dtype)


def _attention_mixer(qkv, q_norm, k_norm, bias):
    b, l, _ = qkv.shape
    outs, lses = zip(*[_dilated_attention(qkv, q_norm, k_norm, bias, gi) for gi in range(ATTN_GROUPS)])
    ts = 1024
    spec = pl.BlockSpec((None, ts, ATTN_OUT), lambda bi, i: (bi, i, 0))
    return pl.pallas_call(
        _attn_merge_kernel,
        grid=(b, l // ts),
        in_specs=[spec] * 6,
        out_specs=spec,
        out_shape=jax.ShapeDtypeStruct((b, l, ATTN_OUT), BF16),
        compiler_params=_params(("parallel", "parallel"), 40),
        name="attn_group_softmax",
    )(*outs, *lses)


def _merge_kernel(ya_ref, yb_ref, yc_ref, gl_ref, pa_ref, pb_ref, pc_ref, g0_ref, g1_ref, g2_ref,
                  b0_ref, b1_ref, b2_ref, o_ref):
    gl = gl_ref[...]
    dot = lambda x, w_ref: jnp.dot(x, w_ref[...], preferred_element_type=F32)
    acc = _sigmoid(dot(gl, g0_ref) + b0_ref[...]) * dot(ya_ref[...], pa_ref)
    acc += _sigmoid(dot(gl, g1_ref) + b1_ref[...]) * dot(yb_ref[...], pb_ref)
    acc += _sigmoid(dot(gl, g2_ref) + b2_ref[...]) * dot(yc_ref[...], pc_ref)
    o_ref[...] = acc.astype(o_ref.dtype)


def _gated_merge(y_a, y_b, y_c, g_low, proj_a, proj_b, proj_c, gate_up, gate_b, tm, tn):
    m = y_a.shape[0]
    d = proj_a.shape[1]
    nj = d // tn
    act = lambda k: pl.BlockSpec((tm, k), lambda i, j: (i, 0))
    wt = lambda k, o: pl.BlockSpec((k, tn), lambda i, j: (0, j + o * nj))
    gate_b2 = gate_b.reshape(1, 3 * d)
    return pl.pallas_call(
        _merge_kernel,
        grid=(m // tm, nj),
        in_specs=[act(y_a.shape[1]), act(y_b.shape[1]), act(y_c.shape[1]), act(g_low.shape[1]),
                  wt(proj_a.shape[0], 0), wt(proj_b.shape[0], 0), wt(proj_c.shape[0], 0),
                  wt(GATE_RANK, 0), wt(GATE_RANK, 1), wt(GATE_RANK, 2),
                  wt(1, 0), wt(1, 1), wt(1, 2)],
        out_specs=pl.BlockSpec((tm, tn), lambda i, j: (i, j)),
        out_shape=jax.ShapeDtypeStruct((m, d), BF16),
        compiler_params=_params(("parallel", "arbitrary"), 56),
        name="gated_merge",
    )(y_a, y_b, y_c, g_low, proj_a, proj_b, proj_c, gate_up, gate_up, gate_up, gate_b2, gate_b2, gate_b2)


def _reorder_w_in(w_in):
    o = np.cumsum((0, POOL_WIDTH, SSD_INNER, SSD_XBC, 2 * SSD_HEADS, ATTN_QKV, GATE_RANK))
    main = jnp.concatenate([w_in[:, o[0]:o[3]], w_in[:, o[4]:o[6]], w_in[:, o[3]:o[4]]], axis=1)
    return jnp.pad(main.astype(BF16), ((0, 0), (0, _WIN_TILES * _WIN_TN - main.shape[1])))


def kernel(x, c, ada_w, ada_b, ada_layer, t5_table, norm_mix, norm_mlp, w_in, pool_w, pool_scale, ssd_conv_w, ssd_conv_b, ssd_dt_bias, ssd_a_log, ssd_d, ssd_norm, q_norm, k_norm, proj_a, proj_b, proj_c, gate_up, gate_b, w_out, mlp_up, mlp_conv_w, mlp_conv_b, mlp_down):
    b, l, d = x.shape
    m = b * l
    mod = _modulation(c, ada_w, ada_b, ada_layer)
    bias = _t5_bias(t5_table)
    for layer in range(DEPTH):
        shift_m, scale_m, gate_m, shift_f, scale_f, gate_f = (mod[layer, :, k] for k in range(N_MOD))
        h = _norm_mod(x, norm_mix[layer], scale_m, shift_m)
        a_in, z, xbc, qkv, g_low, dt_raw = _in_projection(h.reshape(m, d), _reorder_w_in(w_in[layer]), 1024)
        seq = lambda t: t.reshape(b, l, t.shape[-1])
        y_a = _pool_mixer(seq(a_in), pool_w[layer].astype(BF16), pool_scale[layer])
        y_b = _ssd_mixer(seq(z), seq(xbc), seq(dt_raw), ssd_conv_w[layer], ssd_conv_b[layer], ssd_dt_bias[layer],
                         ssd_a_log[layer], ssd_d[layer], ssd_norm[layer])
        y_c = _attention_mixer(seq(qkv), q_norm[layer], k_norm[layer], bias)
        merged = _gated_merge(y_a.reshape(m, -1), y_b.reshape(m, -1), y_c.reshape(m, -1), g_low,
                              proj_a[layer].astype(BF16), proj_b[layer].astype(BF16), proj_c[layer].astype(BF16),
                              gate_up[layer].astype(BF16), gate_b[layer], 1024, 512)
        x = _matmul_residual(merged, w_out[layer].astype(BF16), x.reshape(m, d), gate_m, 1024, 512,
                             "out_proj").reshape(b, l, d)
        h = _norm_mod(x, norm_mlp[layer], scale_f, shift_f)
        up = _matmul(h.reshape(m, d), mlp_up[layer].astype(BF16), 1024, 512, BF16, "ffn_up")
        act = _ffn_act(up.reshape(b, l, -1), mlp_conv_w[layer], mlp_conv_b[layer])
        x = _matmul_residual(act.reshape(m, -1), mlp_down[layer].astype(BF16), x.reshape(m, d), gate_f, 512, 512,
                             "ffn_down").reshape(b, l, d)
    return x
```

```python
import functools

import numpy as np
import jax
import jax.numpy as jnp
from jax import lax
from jax.experimental import pallas as pl
from jax.experimental.pallas import tpu as pltpu

F32 = jnp.float32
BF16 = jnp.bfloat16

D_MODEL = 4096
DEPTH = 4
N_MOD = 6
EPS = 1e-6
NEG_INF = -1e30

POOL_WINDOWS = (2, 4, 8, 16)
POOL_WIDTH = 1536
POOL_GROUP_DIM = 384

SSD_INNER = 2048
SSD_HEADS = 32
SSD_HEAD_DIM = 64
SSD_GROUPS = 4
SSD_STATE = 128
SSD_CHUNK = 128
SSD_BC = 512
SSD_XBC = 3072
SSD_GROUP_WIDTH = SSD_INNER // SSD_GROUPS

DILATIONS = (1, 4, 16)
ATTN_RADIUS = 64
ATTN_GROUPS = 3
ATTN_HEADS = 4
ATTN_HEAD_DIM = 128
ATTN_OUT = 512
ATTN_QKV = 4608
T5_BUCKETS = 32
T5_MAX_DISTANCE = 1024
GATE_RANK = 512
MLP_HIDDEN = 8192

HALO = 16
LANE = 128
MIB = 1024 * 1024


def _params(sem, vmem_mib):
    return pltpu.CompilerParams(dimension_semantics=sem, vmem_limit_bytes=vmem_mib * MIB)


def _sigmoid(x):
    return 1.0 / (1.0 + jnp.exp(-x))


def _silu(x):
    return x * _sigmoid(x)


def _softplus(x):
    return jnp.maximum(x, 0.0) + jnp.log1p(jnp.exp(-jnp.abs(x)))


def _mod_kernel(c_ref, w_ref, b_ref, al_ref, o_ref):
    s = _silu(c_ref[...])
    acc = jnp.dot(s.astype(BF16), w_ref[...].astype(BF16), preferred_element_type=F32)
    base = acc + b_ref[...]
    o_ref[...] = base[None, :, :] + al_ref[...][:, None, :]


def _modulation(c, ada_w, ada_b, ada_layer):
    b = c.shape[0]
    n = ada_w.shape[1]
    tn = 512
    c8 = jnp.zeros((8, D_MODEL), F32).at[:b].set(c)
    out = pl.pallas_call(
        _mod_kernel,
        grid=(n // tn,),
        in_specs=[
            pl.BlockSpec((8, D_MODEL), lambda j: (0, 0)),
            pl.BlockSpec((D_MODEL, tn), lambda j: (0, j)),
            pl.BlockSpec((1, tn), lambda j: (0, j)),
            pl.BlockSpec((DEPTH, tn), lambda j: (0, j)),
        ],
        out_specs=pl.BlockSpec((DEPTH, 8, tn), lambda j: (0, 0, j)),
        out_shape=jax.ShapeDtypeStruct((DEPTH, 8, n), F32),
        compiler_params=_params(("arbitrary",), 40),
        name="adaln_mod",
    )(c8, ada_w, ada_b.reshape(1, n), ada_layer)
    return out[:, :b].reshape(DEPTH, b, N_MOD, 1, D_MODEL)


def _norm_kernel(x_ref, g_ref, sc_ref, sh_ref, o_ref):
    x = x_ref[...]
    ms = jnp.mean(x * x, axis=-1, keepdims=True)
    y = x * lax.rsqrt(ms + EPS) * g_ref[...]
    o_ref[...] = (y * (1.0 + sc_ref[...]) + sh_ref[...]).astype(o_ref.dtype)


def _norm_mod(x, g, scale, shift):
    b, l, d = x.shape
    ts = 512
    return pl.pallas_call(
        _norm_kernel,
        grid=(b, l // ts),
        in_specs=[
            pl.BlockSpec((None, ts, d), lambda bi, i: (bi, i, 0)),
            pl.BlockSpec((1, d), lambda bi, i: (0, 0)),
            pl.BlockSpec((None, 1, d), lambda bi, i: (bi, 0, 0)),
            pl.BlockSpec((None, 1, d), lambda bi, i: (bi, 0, 0)),
        ],
        out_specs=pl.BlockSpec((None, ts, d), lambda bi, i: (bi, i, 0)),
        out_shape=jax.ShapeDtypeStruct((b, l, d), BF16),
        compiler_params=_params(("parallel", "parallel"), 40),
        name="norm_mod",
    )(x, g.reshape(1, d), scale, shift)


def _weight_spec(k, tn, layer, col=lambda j: j):
    return pl.BlockSpec((None, k, tn), lambda i, j: (layer, 0, col(j)))


def _mm_res_kernel(a_ref, b_ref, x_ref, g_ref, o_ref):
    acc = jnp.dot(a_ref[...], b_ref[...], preferred_element_type=F32)
    o_ref[...] = x_ref[...] + g_ref[...] * acc


def _matmul_residual(a, w, layer, x, gate, tm, tn, name):
    m, k = a.shape
    n = w.shape[2]
    blocks_per_batch = (m // gate.shape[0]) // tm
    return pl.pallas_call(
        _mm_res_kernel,
        grid=(m // tm, n // tn),
        in_specs=[
            pl.BlockSpec((tm, k), lambda i, j: (i, 0)),
            _weight_spec(k, tn, layer),
            pl.BlockSpec((tm, tn), lambda i, j: (i, j)),
            pl.BlockSpec((None, 1, tn), lambda i, j: (i // blocks_per_batch, 0, j)),
        ],
        out_specs=pl.BlockSpec((tm, tn), lambda i, j: (i, j)),
        out_shape=jax.ShapeDtypeStruct((m, n), F32),
        compiler_params=_params(("parallel", "arbitrary"), 56),
        name=name,
    )(a, w, x, gate)


_WIN_TN = 512
_WIN_SEGMENTS = (POOL_WIDTH, SSD_INNER, SSD_XBC, ATTN_QKV, GATE_RANK)
_WIN_TILE_START = tuple(int(s) // _WIN_TN for s in np.cumsum((0,) + _WIN_SEGMENTS))
_WIN_TILES = _WIN_TILE_START[-1] + 1
_WIN_QKV = 3
DT_LANES = LANE


def _head_rmsnorm(x, gain):
    out = []
    for h in range(ATTN_HEADS):
        xh = x[:, h * ATTN_HEAD_DIM:(h + 1) * ATTN_HEAD_DIM]
        ms = jnp.mean(xh * xh, axis=-1, keepdims=True)
        out.append(xh * lax.rsqrt(ms + EPS) * gain)
    return jnp.concatenate(out, axis=1)


def _win_kernel(a_ref, b_ref, qg_ref, kg_ref, *o_refs):
    j = pl.program_id(1)
    acc = jnp.dot(a_ref[...], b_ref[...], preferred_element_type=F32)
    q0 = _WIN_TILE_START[_WIN_QKV]
    k0, v0 = q0 + ATTN_GROUPS, q0 + 2 * ATTN_GROUPS
    for s, o_ref in enumerate(o_refs[:-1]):
        lo, hi = (v0, _WIN_TILE_START[s + 1]) if s == _WIN_QKV else (_WIN_TILE_START[s], _WIN_TILE_START[s + 1])

        @pl.when((j >= lo) & (j < hi))
        def _(o_ref=o_ref):
            o_ref[...] = acc.astype(o_ref.dtype)

    @pl.when((j >= q0) & (j < k0))
    def _():
        o_refs[_WIN_QKV][...] = _head_rmsnorm(acc, qg_ref[...] * ATTN_HEAD_DIM ** -0.5).astype(BF16)

    @pl.when((j >= k0) & (j < v0))
    def _():
        o_refs[_WIN_QKV][...] = _head_rmsnorm(acc, kg_ref[...]).astype(BF16)

    @pl.when(j == _WIN_TILES - 1)
    def _():
        o_refs[-1][...] = acc[:, :DT_LANES]


def _in_projection(h, w_in_r, layer, q_norm, k_norm, tm):
    m, k = h.shape

    def seg_spec(s):
        lo, n_tiles = _WIN_TILE_START[s], _WIN_TILE_START[s + 1] - _WIN_TILE_START[s]
        return pl.BlockSpec((tm, _WIN_TN), lambda i, j: (i, jnp.clip(j - lo, 0, n_tiles - 1)))

    out_specs = [seg_spec(s) for s in range(len(_WIN_SEGMENTS))]
    out_specs.append(pl.BlockSpec((tm, DT_LANES), lambda i, j: (i, 0)))
    out_shape = [jax.ShapeDtypeStruct((m, w), BF16) for w in _WIN_SEGMENTS]
    out_shape.append(jax.ShapeDtypeStruct((m, DT_LANES), F32))
    gain_spec = pl.BlockSpec((1, ATTN_HEAD_DIM), lambda i, j: (0, 0))
    return pl.pallas_call(
        _win_kernel,
        grid=(m // tm, _WIN_TILES),
        in_specs=[
            pl.BlockSpec((tm, k), lambda i, j: (i, 0)),
            _weight_spec(k, _WIN_TN, layer),
            gain_spec, gain_spec,
        ],
        out_specs=out_specs,
        out_shape=out_shape,
        compiler_params=_params(("parallel", "arbitrary"), 56),
        name="in_proj",
    )(h, w_in_r, q_norm.reshape(1, ATTN_HEAD_DIM), k_norm.reshape(1, ATTN_HEAD_DIM))


FFN_CONV = 3


def _ffn_up_kernel(hm_ref, hp_ref, hn_ref, wu_ref, wv_ref, cwu_ref, cbu_ref, cwv_ref, cbv_ref, o_ref,
                   a_s, u_s, v_s, *, tm, blocks_per_seq):
    @pl.when(pl.program_id(1) == 0)
    def _():
        pos = pl.program_id(0) % blocks_per_seq
        a_s[0:HALO] = jnp.where(pos > 0, hp_ref[...], jnp.zeros_like(hp_ref))
        a_s[HALO:HALO + tm] = hm_ref[...]
        a_s[HALO + tm:] = jnp.where(pos < blocks_per_seq - 1, hn_ref[...], jnp.zeros_like(hn_ref))

    a = a_s[...]
    u_s[...] = jnp.dot(a, wu_ref[...], preferred_element_type=F32)
    v_s[...] = jnp.dot(a, wv_ref[...], preferred_element_type=F32)

    def conv(s_ref, w_ref, b_ref):
        acc = b_ref[...]
        for k in range(FFN_CONV):
            r0 = HALO + k - FFN_CONV // 2
            acc = acc + s_ref[r0:r0 + tm, :] * w_ref[k:k + 1, :]
        return acc

    o_ref[...] = (_silu(conv(u_s, cwu_ref, cbu_ref)) * conv(v_s, cwv_ref, cbv_ref)).astype(o_ref.dtype)


def _ffn_up_act(h, w_up, layer, conv_w, conv_b, seq_len, tm, tn):
    m, k = h.shape
    hid = w_up.shape[2] // 2
    off = hid // tn
    r = tm // HALO
    last = m // HALO - 1
    conv_b2 = conv_b.reshape(1, 2 * hid)
    cw = lambda o: pl.BlockSpec((FFN_CONV, tn), lambda i, j: (0, j + o))
    cb = lambda o: pl.BlockSpec((1, tn), lambda i, j: (0, j + o))
    ext = tm + 2 * HALO
    return pl.pallas_call(
        functools.partial(_ffn_up_kernel, tm=tm, blocks_per_seq=seq_len // tm),
        grid=(m // tm, hid // tn),
        in_specs=[
            pl.BlockSpec((tm, k), lambda i, j: (i, 0), pipeline_mode=pl.Buffered(1)),
            pl.BlockSpec((HALO, k), lambda i, j: (jnp.maximum(i * r - 1, 0), 0)),
            pl.BlockSpec((HALO, k), lambda i, j: (jnp.minimum((i + 1) * r, last), 0)),
            _weight_spec(k, tn, layer), _weight_spec(k, tn, layer, lambda j: j + off),
            cw(0), cb(0), cw(off), cb(off),
        ],
        out_specs=pl.BlockSpec((tm, tn), lambda i, j: (i, j)),
        out_shape=jax.ShapeDtypeStruct((m, hid), BF16),
        scratch_shapes=[pltpu.VMEM((ext, k), BF16), pltpu.VMEM((ext, tn), F32), pltpu.VMEM((ext, tn), F32)],
        compiler_params=_params(("parallel", "arbitrary"), 56),
        name="ffn_up_conv_act",
    )(h, h, h, w_up, w_up, conv_w, conv_b2, conv_w, conv_b2)


def _halo_specs(ts, width, seq_len, col):
    r = ts // HALO
    last = seq_len // HALO - 1
    return [
        pl.BlockSpec((None, ts, width), lambda b, i, j: (b, i, col(j))),
        pl.BlockSpec((None, HALO, width), lambda b, i, j: (b, jnp.maximum(i * r - 1, 0), col(j))),
        pl.BlockSpec((None, HALO, width), lambda b, i, j: (b, jnp.minimum((i + 1) * r, last), col(j))),
    ]


def _extended(main_ref, prev_ref, next_ref, i, n_tiles):
    prev = jnp.where(i > 0, prev_ref[...].astype(F32), 0.0)
    nxt = jnp.where(i < n_tiles - 1, next_ref[...].astype(F32), 0.0)
    return jnp.concatenate([prev, main_ref[...].astype(F32), nxt], axis=0)


def _shift_rows(ext, k):
    if k == 0:
        return ext
    n = ext.shape[0]
    return pltpu.roll(ext, (-k) % n, axis=0)


def _centre(ext, ts):
    return ext[HALO:HALO + ts]


def _pool_kernel(m_ref, p_ref, n_ref, w_ref, sc_ref, o_ref, *, ts, seq_len):
    i = pl.program_id(1)
    ext = _extended(m_ref, p_ref, n_ref, i, seq_len // ts)
    pos = i * ts + lax.broadcasted_iota(jnp.int32, (ts, 1), 0)
    for gi, win in enumerate(POOL_WINDOWS):
        sl = slice(gi * POOL_GROUP_DIM, (gi + 1) * POOL_GROUP_DIM)
        e = ext[:, sl]
        acc = _shift_rows(e, -1) + e
        w = 2
        while w < win:
            acc = _shift_rows(acc, -(w // 2)) + _shift_rows(acc, w // 2)
            w *= 2
        lo = jnp.maximum(pos - win // 2, 0)
        hi = jnp.minimum(pos + win - win // 2, seq_len)
        cnt = (hi - lo).astype(F32)
        p = (_centre(acc, ts) / cnt - _centre(e, ts)).astype(BF16)
        y = jnp.dot(p, w_ref[gi], preferred_element_type=F32)
        o_ref[:, sl] = (y * sc_ref[:, sl]).astype(o_ref.dtype)


def _pool_mixer(a_in, pool_w, pool_scale):
    b, l, c = a_in.shape
    ts = 512
    kern = functools.partial(_pool_kernel, ts=ts, seq_len=l)
    return pl.pallas_call(
        kern,
        grid=(b, l // ts, 1),
        in_specs=_halo_specs(ts, c, l, lambda j: 0) + [
            pl.BlockSpec((len(POOL_WINDOWS), POOL_GROUP_DIM, POOL_GROUP_DIM), lambda bi, i, j: (0, 0, 0)),
            pl.BlockSpec((1, c), lambda bi, i, j: (0, 0)),
        ],
        out_specs=pl.BlockSpec((None, ts, c), lambda bi, i, j: (bi, i, 0)),
        out_shape=jax.ShapeDtypeStruct((b, l, c), BF16),
        compiler_params=_params(("parallel", "parallel", "arbitrary"), 48),
        name="pool_mixer",
    )(a_in, a_in, a_in, pool_w, pool_scale.reshape(1, c))


def _dwconv(ext, w_ref, b_ref, ts):
    width = w_ref.shape[0]
    acc = None
    for k in range(width):
        term = _centre(_shift_rows(ext, k - width // 2), ts) * w_ref[k:k + 1, :]
        acc = term if acc is None else acc + term
    return acc + b_ref[...]


def _ssd_conv_kernel(m_ref, p_ref, n_ref, w_ref, b_ref, o_ref, *, ts, seq_len):
    ext = _extended(m_ref, p_ref, n_ref, pl.program_id(1), seq_len // ts)
    o_ref[...] = _silu(_dwconv(ext, w_ref, b_ref, ts)).astype(o_ref.dtype)


def _ssd_conv(xbc, conv_w, conv_b):
    b, l, c = xbc.shape
    ts, tc = 512, 512
    kern = functools.partial(_ssd_conv_kernel, ts=ts, seq_len=l)
    return pl.pallas_call(
        kern,
        grid=(b, l // ts, c // tc),
        in_specs=_halo_specs(ts, tc, l, lambda j: j) + [
            pl.BlockSpec((conv_w.shape[0], tc), lambda bi, i, j: (0, j)),
            pl.BlockSpec((1, tc), lambda bi, i, j: (0, j)),
        ],
        out_specs=pl.BlockSpec((None, ts, tc), lambda bi, i, j: (bi, i, j)),
        out_shape=jax.ShapeDtypeStruct((b, l, c), BF16),
        compiler_params=_params(("parallel", "parallel", "arbitrary"), 32),
        name="ssd_conv",
    )(xbc, xbc, xbc, conv_w, conv_b.reshape(1, c))


def _split_dot(lhs, rhs, terms, split_lhs):
    x = lhs if split_lhs else rhs
    acc = None
    for _ in range(terms):
        hi = x.astype(BF16)
        part = (jnp.dot(hi, rhs, preferred_element_type=F32) if split_lhs
                else jnp.dot(lhs, hi, preferred_element_type=F32))
        acc = part if acc is None else acc + part
        x = x - hi.astype(F32)
    return acc


def _ssd_kernel(xs_ref, bm_ref, cm_ref, dtr_ref, dtb_ref, alog_ref, e_ref, *rest, reverse, final):
    if final:
        yf_ref, z_ref, dsk_ref, ng_ref, o_ref, st_ref = rest
    else:
        o_ref, st_ref = rest
    tc = SSD_CHUNK
    gw = SSD_GROUP_WIDTH

    @pl.when(pl.program_id(1) == 0)
    def _():
        st_ref[...] = jnp.zeros_like(st_ref)

    lane0 = SSD_HEADS if reverse else 0
    dt = _softplus(dtr_ref[...] + dtb_ref[...])
    dta = dt * (-jnp.exp(alog_ref[...]))
    ti = lax.broadcasted_iota(jnp.int32, (tc, tc), 0)
    ui = lax.broadcasted_iota(jnp.int32, (tc, tc), 1)
    causal = (ui >= ti) if reverse else (ui <= ti)
    cs = _split_dot(jnp.where(causal, 1.0, 0.0).astype(BF16), dta, 3, split_lhs=False)
    total = cs[0:1] if reverse else cs[tc - 1:tc]
    to_end = jnp.exp(total - cs)
    from_start = jnp.exp(cs)
    chunk_decay = jnp.broadcast_to(jnp.exp(total), (8, LANE))
    stacked = jnp.concatenate([dt, dt * to_end, from_start, chunk_decay], axis=0)
    wide = _split_dot(stacked, e_ref[...], 2, split_lhs=True)
    dt_e, w1_e, fs_e, cd_e = wide[0:tc], wide[tc:2 * tc], wide[2 * tc:3 * tc], wide[3 * tc:3 * tc + 1]

    xs = xs_ref[...].astype(F32)
    xdt = (xs * dt_e).astype(BF16)
    xw = (xs * w1_e).astype(BF16)
    cs_t = cs.T
    lane = lax.broadcasted_iota(jnp.int32, (tc, LANE), 1)
    first_head = lane < SSD_HEAD_DIM

    for g in range(SSD_GROUPS):
        gs = slice(g * gw, (g + 1) * gw)
        bm = bm_ref[:, g * SSD_STATE:(g + 1) * SSD_STATE]
        cm = cm_ref[:, g * SSD_STATE:(g + 1) * SSD_STATE]
        cb = lax.dot_general(cm, bm, (((1,), (1,)), ((), ())), preferred_element_type=F32)
        bm_t = bm.astype(F32).T.astype(BF16)
        s_prev = st_ref[g]
        y_off = jnp.dot(cm, s_prev.astype(BF16), preferred_element_type=F32) * fs_e[:, gs]
        s_chunk = jnp.dot(bm_t, xw[:, gs], preferred_element_type=F32)
        st_ref[g] = s_prev * cd_e[:, gs] + s_chunk
        parts = []
        for pr in range(gw // LANE):
            h0 = g * (gw // SSD_HEAD_DIM) + 2 * pr
            ms = []
            for h in (h0, h0 + 1):
                col = cs[:, lane0 + h:lane0 + h + 1]
                row = cs_t[lane0 + h:lane0 + h + 1, :]
                decay = jnp.exp(jnp.where(causal, col - row, NEG_INF))
                ms.append((cb * decay).astype(BF16))
            xp = xdt[:, h0 * SSD_HEAD_DIM:(h0 + 2) * SSD_HEAD_DIM]
            zero = jnp.zeros_like(xp)
            rhs = jnp.concatenate([jnp.where(first_head, xp, zero), jnp.where(first_head, zero, xp)], axis=0)
            y_diag = jnp.dot(jnp.concatenate(ms, axis=1), rhs, preferred_element_type=F32)
            parts.append(y_diag + y_off[:, pr * LANE:(pr + 1) * LANE])
        y = jnp.concatenate(parts, axis=1)
        if final:
            y = y + yf_ref[:, gs] + xs[:, gs] * dsk_ref[:, gs]
            y = y * _silu(z_ref[:, gs].astype(F32))
            y = y * lax.rsqrt(jnp.mean(y * y, axis=-1, keepdims=True) + EPS)
            o_ref[:, gs] = (y * ng_ref[:, gs]).astype(o_ref.dtype)
        else:
            o_ref[:, gs] = y


def _head_expander(reverse):
    e = np.zeros((LANE, SSD_INNER), np.float32)
    lane0 = SSD_HEADS if reverse else 0
    for h in range(SSD_HEADS):
        e[lane0 + h, h * SSD_HEAD_DIM:(h + 1) * SSD_HEAD_DIM] = 1.0
    return jnp.asarray(e, BF16)


def _ssd_pass(xbc_act, dt_raw, dt_bias_row, a_log_row, reverse, final_args=None):
    b, l, _ = xbc_act.shape
    tc = SSD_CHUNK
    nc = l // tc
    final = final_args is not None
    chunk = (lambda c: nc - 1 - c) if reverse else (lambda c: c)
    seq = lambda width, colblk: pl.BlockSpec((None, tc, width), lambda bi, c: (bi, chunk(c), colblk))
    row = lambda width: pl.BlockSpec((1, width), lambda bi, c: (0, 0))
    in_specs = [
        seq(SSD_INNER, 0),
        seq(SSD_BC, SSD_INNER // SSD_BC),
        seq(SSD_BC, SSD_INNER // SSD_BC + 1),
        seq(DT_LANES, 0),
        row(DT_LANES), row(DT_LANES),
        pl.BlockSpec((LANE, SSD_INNER), lambda bi, c: (0, 0)),
    ]
    args = [xbc_act, xbc_act, xbc_act, dt_raw, dt_bias_row, a_log_row, _head_expander(reverse)]
    if final:
        y_fwd, z, d_skip_row, norm_row = final_args
        in_specs += [seq(SSD_INNER, 0), seq(SSD_INNER, 0), row(SSD_INNER), row(SSD_INNER)]
        args += [y_fwd, z, d_skip_row, norm_row]
    return pl.pallas_call(
        functools.partial(_ssd_kernel, reverse=reverse, final=final),
        grid=(b, nc),
        in_specs=in_specs,
        out_specs=seq(SSD_INNER, 0),
        out_shape=jax.ShapeDtypeStruct((b, l, SSD_INNER), BF16 if final else F32),
        scratch_shapes=[pltpu.VMEM((SSD_GROUPS, SSD_STATE, SSD_GROUP_WIDTH), F32)],
        compiler_params=_params(("parallel", "arbitrary"), 32),
        name="ssd_bwd_final" if final else "ssd_fwd",
    )(*args)


def _ssd_mixer(z, xbc, dt_raw, conv_w, conv_b, dt_bias, a_log, d_skip, norm_g):
    xbc_act = _ssd_conv(xbc, conv_w, conv_b)
    pad = DT_LANES - 2 * SSD_HEADS
    dtb = jnp.pad(dt_bias.reshape(1, 2 * SSD_HEADS), ((0, 0), (0, pad)))
    alog = jnp.pad(a_log.reshape(1, 2 * SSD_HEADS), ((0, 0), (0, pad)))
    d_row = jnp.repeat(d_skip, SSD_HEAD_DIM).reshape(1, SSD_INNER)
    y_fwd = _ssd_pass(xbc_act, dt_raw, dtb, alog, reverse=False)
    return _ssd_pass(xbc_act, dt_raw, dtb, alog, reverse=True,
                     final_args=(y_fwd, z, d_row, norm_g.reshape(1, SSD_INNER)))


ATTN_QB = 128
ATTN_KB = ATTN_QB + 2 * ATTN_RADIUS


def _t5_bucket_table():
    half = T5_BUCKETS // 2
    max_exact = half // 2
    delta = np.arange(ATTN_KB)[None, :] - ATTN_RADIUS - np.arange(ATTN_QB)[:, None]
    out = []
    for dil in DILATIONS:
        rel = delta * dil
        n = np.abs(rel)
        large = max_exact + (np.log(np.maximum(n, max_exact) / max_exact) / np.log(T5_MAX_DISTANCE / max_exact)
                             * (half - max_exact)).astype(np.int32)
        large = np.minimum(large, half - 1)
        out.append((rel > 0).astype(np.int32) * half + np.where(n < max_exact, n, large).astype(np.int32))
    return np.stack(out).astype(np.int32)


def _bias_kernel(tab_ref, bkt_ref, o_ref):
    g = pl.program_id(0)
    bk = bkt_ref[...]
    for h in range(ATTN_HEADS):
        acc = jnp.zeros(bk.shape, F32)
        for bucket in range(T5_BUCKETS):
            acc = jnp.where(bk == bucket, tab_ref[bucket, g * ATTN_HEADS + h], acc)
        o_ref[h] = acc


def _t5_bias(t5_table):
    return pl.pallas_call(
        _bias_kernel,
        grid=(ATTN_GROUPS,),
        in_specs=[
            pl.BlockSpec(memory_space=pltpu.SMEM),
            pl.BlockSpec((None, ATTN_QB, ATTN_KB), lambda g: (g, 0, 0)),
        ],
        out_specs=pl.BlockSpec((None, ATTN_HEADS, ATTN_QB, ATTN_KB), lambda g: (g, 0, 0, 0)),
        out_shape=jax.ShapeDtypeStruct((ATTN_GROUPS, ATTN_HEADS, ATTN_QB, ATTN_KB), F32),
        compiler_params=_params(("arbitrary",), 16),
        name="t5_bias",
    )(t5_table, jnp.asarray(_t5_bucket_table()))


ATTN_CHUNK = 1024


_HIGH_HALF = 0xFFFF0000


def _attn_fill(dst, row0, src_ref, col0, n_rows, low_ref=None, low_col0=0):
    step = min(n_rows, LANE)

    def body(c, carry):
        r = pl.multiple_of(c * step, step)
        for h in range(ATTN_HEADS):
            x = src_ref[pl.ds(r, step), col0 + h * ATTN_HEAD_DIM:col0 + (h + 1) * ATTN_HEAD_DIM].astype(F32)
            if low_ref is not None:
                y = low_ref[pl.ds(r, step), low_col0 + h * ATTN_HEAD_DIM:low_col0 + (h + 1) * ATTN_HEAD_DIM]
                x = lax.bitcast_convert_type(x, jnp.uint32) | (lax.bitcast_convert_type(y.astype(F32), jnp.uint32) >> 16)
            dst[h, pl.ds(row0 + r, step), :] = x
        return carry

    lax.fori_loop(0, n_rows // step, body, 0)


def _attn_kernel(q_ref, k_ref, v_ref, *rest, seq_len):
    halos = rest[:4 * ATTN_GROUPS]
    bias_ref, o_ref, qn_s, kv_s, m_s, d_s, n_s = rest[4 * ATTN_GROUPS:]
    i = pl.program_id(1)
    n_chunks = seq_len // ATTN_CHUNK
    r_ = ATTN_RADIUS
    ones = jnp.ones((ATTN_KB, ATTN_HEAD_DIM), BF16)
    for g, dil in enumerate(DILATIONS):
        hb = r_ * dil
        n_strided = ATTN_CHUNK // dil
        qb = min(ATTN_QB, n_strided)
        kb = qb + 2 * r_
        nblk = n_strided // qb
        kp_ref, kn_ref, vp_ref, vn_ref = halos[4 * g:4 * g + 4]
        col = g * ATTN_OUT
        _attn_fill(qn_s, 0, q_ref, col, ATTN_CHUNK)
        _attn_fill(kv_s, 0, kp_ref, 0, hb, vp_ref, 0)
        _attn_fill(kv_s, hb, k_ref, col, ATTN_CHUNK, v_ref, col)
        _attn_fill(kv_s, hb + ATTN_CHUNK, kn_ref, 0, hb, vn_ref, 0)

        qq = lax.broadcasted_iota(jnp.int32, (qb, kb), 0)
        kk = lax.broadcasted_iota(jnp.int32, (qb, kb), 1)
        band = jnp.abs(kk - r_ - qq) <= r_

        def block(it, carry, g=g, dil=dil, qb=qb, kb=kb, nblk=nblk, band=band, kk=kk):
            res = it // nblk
            blk = it % nblk
            start = res + blk * (qb * dil)
            lo = jnp.where((i == 0) & (blk == 0), r_, 0)
            hi = jnp.where((i == n_chunks - 1) & (blk == nblk - 1), qb + r_, kb)
            valid = band & (kk >= lo) & (kk < hi)
            for h in range(ATTN_HEADS):
                q = qn_s[h, pl.ds(start, qb, stride=dil), :].astype(BF16)
                kv = kv_s[h, pl.ds(start, kb, stride=dil), :]
                k = lax.bitcast_convert_type(kv & jnp.uint32(_HIGH_HALF), F32).astype(BF16)
                v = lax.bitcast_convert_type(kv << 16, F32).astype(BF16)
                s = lax.dot_general(q, k, (((1,), (1,)), ((), ())), preferred_element_type=F32)
                s = jnp.where(valid, s + bias_ref[g, h, :qb, :kb], NEG_INF)
                mx = jnp.max(s, axis=-1, keepdims=True)
                ex = jnp.exp(s - mx).astype(BF16)
                nd = jnp.dot(ex, jnp.concatenate([v, ones[:kb]], axis=1), preferred_element_type=F32)
                num, den = nd[:, :ATTN_HEAD_DIM], nd[:, ATTN_HEAD_DIM:]
                mx = jnp.broadcast_to(mx, (qb, ATTN_HEAD_DIM))
                rows = pl.ds(start, qb, stride=dil)
                if g > 0:
                    m_old = m_s[h, rows, :]
                    m_new = jnp.maximum(m_old, mx)
                    w_old = jnp.exp(m_old - m_new)
                    w_new = jnp.exp(mx - m_new)
                    num = n_s[h, rows, :] * w_old + num * w_new
                    den = d_s[h, rows, :] * w_old + den * w_new
                    mx = m_new
                m_s[h, rows, :] = mx
                d_s[h, rows, :] = den
                n_s[h, rows, :] = num
            return carry

        lax.fori_loop(0, dil * nblk, block, 0, unroll=2)

    def finish(c, carry):
        r = pl.multiple_of(c * LANE, LANE)
        for h in range(ATTN_HEADS):
            out = n_s[h, pl.ds(r, LANE), :] / d_s[h, pl.ds(r, LANE), :]
            o_ref[pl.ds(r, LANE), h * ATTN_HEAD_DIM:(h + 1) * ATTN_HEAD_DIM] = out.astype(o_ref.dtype)
        return carry

    lax.fori_loop(0, ATTN_CHUNK // LANE, finish, 0)


def _attention_mixer(qkv, bias):
    b, l, _ = qkv.shape
    c = ATTN_CHUNK
    third = ATTN_QKV // 3
    main = lambda which: pl.BlockSpec((None, c, third), lambda bi, i: (bi, i, which))
    in_specs = [main(0), main(1), main(2)]
    args = [qkv, qkv, qkv]
    for g, dil in enumerate(DILATIONS):
        hb = ATTN_RADIUS * dil
        per = c // hb
        last = l // hb - 1
        mode = dict(pipeline_mode=pl.Buffered(1)) if hb >= c else {}
        for which in (1, 2):
            colblk = which * ATTN_GROUPS + g
            in_specs.append(pl.BlockSpec((None, hb, ATTN_OUT),
                                         lambda bi, i, per=per, colblk=colblk: (bi, jnp.maximum(i * per - 1, 0), colblk),
                                         **mode))
            in_specs.append(pl.BlockSpec((None, hb, ATTN_OUT),
                                         lambda bi, i, per=per, last=last, colblk=colblk:
                                         (bi, jnp.minimum((i + 1) * per, last), colblk), **mode))
            args += [qkv, qkv]
    in_specs.append(pl.BlockSpec((ATTN_GROUPS, ATTN_HEADS, ATTN_QB, ATTN_KB), lambda bi, i: (0, 0, 0, 0),
                                 pipeline_mode=pl.Buffered(1)))
    args.append(bias)
    ext = c + 2 * ATTN_RADIUS * DILATIONS[-1]
    head_rows = lambda n: pltpu.VMEM((ATTN_HEADS, n, ATTN_HEAD_DIM), F32)
    return pl.pallas_call(
        functools.partial(_attn_kernel, seq_len=l),
        grid=(b, l // c),
        in_specs=in_specs,
        out_specs=pl.BlockSpec((None, c, ATTN_OUT), lambda bi, i: (bi, i, 0)),
        out_shape=jax.ShapeDtypeStruct((b, l, ATTN_OUT), BF16),
        scratch_shapes=[head_rows(c), pltpu.VMEM((ATTN_HEADS, ext, ATTN_HEAD_DIM), jnp.uint32),
                        head_rows(c), head_rows(c), head_rows(c)],
        compiler_params=_params(("parallel", "arbitrary"), 56),
        name="dilated_attention",
    )(*args)


def _merge_kernel(ya_ref, yb_ref, yc_ref, gl_ref, pa_ref, pb_ref, pc_ref, g0_ref, g1_ref, g2_ref,
                  b0_ref, b1_ref, b2_ref, o_ref):
    gl = gl_ref[...]
    dot = lambda x, w_ref: jnp.dot(x, w_ref[...], preferred_element_type=F32)
    acc = _sigmoid(dot(gl, g0_ref) + b0_ref[...]) * dot(ya_ref[...], pa_ref)
    acc += _sigmoid(dot(gl, g1_ref) + b1_ref[...]) * dot(yb_ref[...], pb_ref)
    acc += _sigmoid(dot(gl, g2_ref) + b2_ref[...]) * dot(yc_ref[...], pc_ref)
    o_ref[...] = acc.astype(o_ref.dtype)


def _gated_merge(y_a, y_b, y_c, g_low, proj_a, proj_b, proj_c, gate_up, gate_b, layer, tm, tn):
    m = y_a.shape[0]
    d = proj_a.shape[2]
    nj = d // tn
    act = lambda k: pl.BlockSpec((tm, k), lambda i, j: (i, 0))
    wt = lambda k, o: _weight_spec(k, tn, layer, lambda j: j + o * nj)
    gate_b2 = gate_b.reshape(DEPTH, 1, 3 * d)
    return pl.pallas_call(
        _merge_kernel,
        grid=(m // tm, nj),
        in_specs=[act(y_a.shape[1]), act(y_b.shape[1]), act(y_c.shape[1]), act(g_low.shape[1]),
                  wt(proj_a.shape[1], 0), wt(proj_b.shape[1], 0), wt(proj_c.shape[1], 0),
                  wt(GATE_RANK, 0), wt(GATE_RANK, 1), wt(GATE_RANK, 2),
                  wt(1, 0), wt(1, 1), wt(1, 2)],
        out_specs=pl.BlockSpec((tm, tn), lambda i, j: (i, j)),
        out_shape=jax.ShapeDtypeStruct((m, d), BF16),
        compiler_params=_params(("parallel", "arbitrary"), 56),
        name="gated_merge",
    )(y_a, y_b, y_c, g_low, proj_a, proj_b, proj_c, gate_up, gate_up, gate_up, gate_b2, gate_b2, gate_b2)


def _reorder_w_in(w_in):
    o = np.cumsum((0, POOL_WIDTH, SSD_INNER, SSD_XBC, 2 * SSD_HEADS, ATTN_QKV, GATE_RANK))
    main = jnp.concatenate([w_in[..., o[0]:o[3]], w_in[..., o[4]:o[6]], w_in[..., o[3]:o[4]]], axis=-1)
    return jnp.pad(main.astype(BF16), ((0, 0), (0, 0), (0, _WIN_TILES * _WIN_TN - main.shape[-1])))


def kernel(x, c, ada_w, ada_b, ada_layer, t5_table, norm_mix, norm_mlp, w_in, pool_w, pool_scale, ssd_conv_w, ssd_conv_b, ssd_dt_bias, ssd_a_log, ssd_d, ssd_norm, q_norm, k_norm, proj_a, proj_b, proj_c, gate_up, gate_b, w_out, mlp_up, mlp_conv_w, mlp_conv_b, mlp_down):
    b, l, d = x.shape
    m = b * l
    mod = _modulation(c, ada_w, ada_b, ada_layer)
    bias = _t5_bias(t5_table)
    w_in_r = _reorder_w_in(w_in)
    pool_w, proj_a, proj_b, proj_c, gate_up, w_out, mlp_up, mlp_down = (
        w.astype(BF16) for w in (pool_w, proj_a, proj_b, proj_c, gate_up, w_out, mlp_up, mlp_down))
    for layer in range(DEPTH):
        shift_m, scale_m, gate_m, shift_f, scale_f, gate_f = (mod[layer, :, k] for k in range(N_MOD))
        h = _norm_mod(x, norm_mix[layer], scale_m, shift_m)
        a_in, z, xbc, qkv, g_low, dt_raw = _in_projection(h.reshape(m, d), w_in_r, layer, q_norm[layer],
                                                          k_norm[layer], 1024)
        seq = lambda t: t.reshape(b, l, t.shape[-1])
        y_a = _pool_mixer(seq(a_in), pool_w[layer], pool_scale[layer])
        y_b = _ssd_mixer(seq(z), seq(xbc), seq(dt_raw), ssd_conv_w[layer], ssd_conv_b[layer], ssd_dt_bias[layer],
                         ssd_a_log[layer], ssd_d[layer], ssd_norm[layer])
        y_c = _attention_mixer(seq(qkv), bias)
        merged = _gated_merge(y_a.reshape(m, -1), y_b.reshape(m, -1), y_c.reshape(m, -1), g_low,
                              proj_a, proj_b, proj_c, gate_up, gate_b, layer, 1024, 512)
        x = _matmul_residual(merged, w_out, layer, x.reshape(m, d), gate_m, 1024, 512, "out_proj").reshape(b, l, d)
        h = _norm_mod(x, norm_mlp[layer], scale_f, shift_f)
        act = _ffn_up_act(h.reshape(m, d), mlp_up, layer, mlp_conv_w[layer], mlp_conv_b[layer], l, 1024, 512)
        x = _matmul_residual(act, mlp_down, layer, x.reshape(m, d), gate_f, 512, 512, "ffn_down").reshape(b, l, d)
    return x
```

```python
import functools

import numpy as np
import jax
import jax.numpy as jnp
from jax import lax
from jax.experimental import pallas as pl
from jax.experimental.pallas import tpu as pltpu

F32 = jnp.float32
BF16 = jnp.bfloat16

D_MODEL = 4096
DEPTH = 4
N_MOD = 6
EPS = 1e-6
NEG_INF = -1e30
LOG2_E = 1.4426950408889634

POOL_WINDOWS = (2, 4, 8, 16)
POOL_WIDTH = 1536
POOL_GROUP_DIM = 384

SSD_INNER = 2048
SSD_HEADS = 32
SSD_HEAD_DIM = 64
SSD_GROUPS = 4
SSD_STATE = 128
SSD_CHUNK = 128
SSD_BC = 512
SSD_XBC = 3072
SSD_GROUP_WIDTH = SSD_INNER // SSD_GROUPS

DILATIONS = (1, 4, 16)
ATTN_RADIUS = 64
ATTN_GROUPS = 3
ATTN_HEADS = 4
ATTN_HEAD_DIM = 128
ATTN_OUT = 512
ATTN_QKV = 4608
T5_BUCKETS = 32
T5_MAX_DISTANCE = 1024
GATE_RANK = 512
MLP_HIDDEN = 8192

HALO = 16
LANE = 128
MIB = 1024 * 1024


def _params(sem, vmem_mib):
    return pltpu.CompilerParams(dimension_semantics=sem, vmem_limit_bytes=vmem_mib * MIB)


def _sigmoid(x):
    return 1.0 / (1.0 + jnp.exp(-x))


def _silu(x):
    return x * _sigmoid(x)


def _softplus(x):
    return jnp.maximum(x, 0.0) + jnp.log1p(jnp.exp(-jnp.abs(x)))


def _mod_kernel(c_ref, w_ref, b_ref, al_ref, o_ref):
    s = _silu(c_ref[...])
    acc = jnp.dot(s.astype(BF16), w_ref[...].astype(BF16), preferred_element_type=F32)
    base = acc + b_ref[...]
    o_ref[...] = base[None, :, :] + al_ref[...][:, None, :]


def _modulation(c, ada_w, ada_b, ada_layer):
    b = c.shape[0]
    n = ada_w.shape[1]
    tn = 512
    c8 = jnp.zeros((8, D_MODEL), F32).at[:b].set(c)
    out = pl.pallas_call(
        _mod_kernel,
        grid=(n // tn,),
        in_specs=[
            pl.BlockSpec((8, D_MODEL), lambda j: (0, 0)),
            pl.BlockSpec((D_MODEL, tn), lambda j: (0, j)),
            pl.BlockSpec((1, tn), lambda j: (0, j)),
            pl.BlockSpec((DEPTH, tn), lambda j: (0, j)),
        ],
        out_specs=pl.BlockSpec((DEPTH, 8, tn), lambda j: (0, 0, j)),
        out_shape=jax.ShapeDtypeStruct((DEPTH, 8, n), F32),
        compiler_params=_params(("arbitrary",), 40),
        name="adaln_mod",
    )(c8, ada_w, ada_b.reshape(1, n), ada_layer)
    return out[:, :b].reshape(DEPTH, b, N_MOD, 1, D_MODEL)


def _norm_kernel(x_ref, g_ref, sc_ref, sh_ref, o_ref):
    x = x_ref[...]
    ms = jnp.mean(x * x, axis=-1, keepdims=True)
    y = x * lax.rsqrt(ms + EPS) * g_ref[...]
    o_ref[...] = (y * (1.0 + sc_ref[...]) + sh_ref[...]).astype(o_ref.dtype)


def _norm_mod(x, g, scale, shift):
    b, l, d = x.shape
    ts = 512
    return pl.pallas_call(
        _norm_kernel,
        grid=(b, l // ts),
        in_specs=[
            pl.BlockSpec((None, ts, d), lambda bi, i: (bi, i, 0)),
            pl.BlockSpec((1, d), lambda bi, i: (0, 0)),
            pl.BlockSpec((None, 1, d), lambda bi, i: (bi, 0, 0)),
            pl.BlockSpec((None, 1, d), lambda bi, i: (bi, 0, 0)),
        ],
        out_specs=pl.BlockSpec((None, ts, d), lambda bi, i: (bi, i, 0)),
        out_shape=jax.ShapeDtypeStruct((b, l, d), BF16),
        compiler_params=_params(("parallel", "parallel"), 40),
        name="norm_mod",
    )(x, g.reshape(1, d), scale, shift)


def _weight_spec(k, tn, layer, col=lambda j: j):
    return pl.BlockSpec((None, k, tn), lambda i, j: (layer, 0, col(j)))


def _mm_res_kernel(a_ref, b_ref, x_ref, g_ref, o_ref):
    acc = jnp.dot(a_ref[...], b_ref[...], preferred_element_type=F32)
    o_ref[...] = x_ref[...] + g_ref[...] * acc


def _matmul_residual(a, w, layer, x, gate, tm, tn, name):
    m, k = a.shape
    n = w.shape[2]
    blocks_per_batch = (m // gate.shape[0]) // tm
    return pl.pallas_call(
        _mm_res_kernel,
        grid=(m // tm, n // tn),
        in_specs=[
            pl.BlockSpec((tm, k), lambda i, j: (i, 0)),
            _weight_spec(k, tn, layer),
            pl.BlockSpec((tm, tn), lambda i, j: (i, j)),
            pl.BlockSpec((None, 1, tn), lambda i, j: (i // blocks_per_batch, 0, j)),
        ],
        out_specs=pl.BlockSpec((tm, tn), lambda i, j: (i, j)),
        out_shape=jax.ShapeDtypeStruct((m, n), F32),
        compiler_params=_params(("parallel", "arbitrary"), 56),
        name=name,
    )(a, w, x, gate)


_WIN_TN = 512
_WIN_SEGMENTS = (POOL_WIDTH, SSD_INNER, SSD_XBC, ATTN_QKV, GATE_RANK)
_WIN_TILE_START = tuple(int(s) // _WIN_TN for s in np.cumsum((0,) + _WIN_SEGMENTS))
_WIN_TILES = _WIN_TILE_START[-1] + 1
_WIN_QKV = 3
DT_LANES = LANE


def _head_rmsnorm(x, gain):
    out = []
    for h in range(ATTN_HEADS):
        xh = x[:, h * ATTN_HEAD_DIM:(h + 1) * ATTN_HEAD_DIM]
        ms = jnp.mean(xh * xh, axis=-1, keepdims=True)
        out.append(xh * lax.rsqrt(ms + EPS) * gain)
    return jnp.concatenate(out, axis=1)


def _win_kernel(a_ref, b_ref, qg_ref, kg_ref, *rest):
    o_refs, acc_s = rest[:-1], rest[-1]
    j = pl.program_id(1)
    q0 = _WIN_TILE_START[_WIN_QKV]
    k0, v0 = q0 + ATTN_GROUPS, q0 + 2 * ATTN_GROUPS

    @pl.when(j < _WIN_TILES - 1)
    def _():
        acc_s[...] = jnp.dot(a_ref[...], b_ref[...], preferred_element_type=F32)

    @pl.when(j == _WIN_TILES - 1)
    def _():
        o_refs[-1][...] = jnp.dot(a_ref[...], b_ref[:, :DT_LANES], preferred_element_type=F32)

    for s, o_ref in enumerate(o_refs[:-1]):
        lo, hi = (v0, _WIN_TILE_START[s + 1]) if s == _WIN_QKV else (_WIN_TILE_START[s], _WIN_TILE_START[s + 1])

        @pl.when((j >= lo) & (j < hi))
        def _(o_ref=o_ref):
            o_ref[...] = acc_s[...].astype(o_ref.dtype)

    @pl.when((j >= q0) & (j < k0))
    def _():
        o_refs[_WIN_QKV][...] = _head_rmsnorm(acc_s[...], qg_ref[...] * ATTN_HEAD_DIM ** -0.5).astype(BF16)

    @pl.when((j >= k0) & (j < v0))
    def _():
        o_refs[_WIN_QKV][...] = _head_rmsnorm(acc_s[...], kg_ref[...]).astype(BF16)


def _in_projection(h, w_in_r, layer, q_norm, k_norm, tm):
    m, k = h.shape

    def seg_spec(s):
        lo, n_tiles = _WIN_TILE_START[s], _WIN_TILE_START[s + 1] - _WIN_TILE_START[s]
        return pl.BlockSpec((tm, _WIN_TN), lambda i, j: (i, jnp.clip(j - lo, 0, n_tiles - 1)))

    out_specs = [seg_spec(s) for s in range(len(_WIN_SEGMENTS))]
    out_specs.append(pl.BlockSpec((tm, DT_LANES), lambda i, j: (i, 0)))
    out_shape = [jax.ShapeDtypeStruct((m, w), BF16) for w in _WIN_SEGMENTS]
    out_shape.append(jax.ShapeDtypeStruct((m, DT_LANES), F32))
    gain_spec = pl.BlockSpec((1, ATTN_HEAD_DIM), lambda i, j: (0, 0))
    return pl.pallas_call(
        _win_kernel,
        grid=(m // tm, _WIN_TILES),
        in_specs=[
            pl.BlockSpec((tm, k), lambda i, j: (i, 0)),
            _weight_spec(k, _WIN_TN, layer),
            gain_spec, gain_spec,
        ],
        out_specs=out_specs,
        out_shape=out_shape,
        scratch_shapes=[pltpu.VMEM((tm, _WIN_TN), F32)],
        compiler_params=_params(("parallel", "arbitrary"), 56),
        name="in_proj",
    )(h, w_in_r, q_norm.reshape(1, ATTN_HEAD_DIM), k_norm.reshape(1, ATTN_HEAD_DIM))


FFN_CONV = 3


def _ffn_up_kernel(hm_ref, hp_ref, hn_ref, wu_ref, wv_ref, cwu_ref, cbu_ref, cwv_ref, cbv_ref, o_ref,
                   a_s, u_s, v_s, *, tm, blocks_per_seq):
    @pl.when(pl.program_id(1) == 0)
    def _():
        pos = pl.program_id(0) % blocks_per_seq
        a_s[0:HALO] = jnp.where(pos > 0, hp_ref[...], jnp.zeros_like(hp_ref))
        a_s[HALO:HALO + tm] = hm_ref[...]
        a_s[HALO + tm:] = jnp.where(pos < blocks_per_seq - 1, hn_ref[...], jnp.zeros_like(hn_ref))

    a = a_s[...]
    u_s[...] = jnp.dot(a, wu_ref[...], preferred_element_type=F32)
    v_s[...] = jnp.dot(a, wv_ref[...], preferred_element_type=F32)

    def conv(s_ref, w_ref, b_ref):
        acc = b_ref[...]
        for k in range(FFN_CONV):
            r0 = HALO + k - FFN_CONV // 2
            acc = acc + s_ref[r0:r0 + tm, :] * w_ref[k:k + 1, :]
        return acc

    o_ref[...] = (_silu(conv(u_s, cwu_ref, cbu_ref)) * conv(v_s, cwv_ref, cbv_ref)).astype(o_ref.dtype)


def _ffn_up_act(h, w_up, layer, conv_w, conv_b, seq_len, tm, tn):
    m, k = h.shape
    hid = w_up.shape[2] // 2
    off = hid // tn
    r = tm // HALO
    last = m // HALO - 1
    conv_b2 = conv_b.reshape(1, 2 * hid)
    cw = lambda o: pl.BlockSpec((FFN_CONV, tn), lambda i, j: (0, j + o))
    cb = lambda o: pl.BlockSpec((1, tn), lambda i, j: (0, j + o))
    ext = tm + 2 * HALO
    return pl.pallas_call(
        functools.partial(_ffn_up_kernel, tm=tm, blocks_per_seq=seq_len // tm),
        grid=(m // tm, hid // tn),
        in_specs=[
            pl.BlockSpec((tm, k), lambda i, j: (i, 0), pipeline_mode=pl.Buffered(1)),
            pl.BlockSpec((HALO, k), lambda i, j: (jnp.maximum(i * r - 1, 0), 0)),
            pl.BlockSpec((HALO, k), lambda i, j: (jnp.minimum((i + 1) * r, last), 0)),
            _weight_spec(k, tn, layer), _weight_spec(k, tn, layer, lambda j: j + off),
            cw(0), cb(0), cw(off), cb(off),
        ],
        out_specs=pl.BlockSpec((tm, tn), lambda i, j: (i, j)),
        out_shape=jax.ShapeDtypeStruct((m, hid), BF16),
        scratch_shapes=[pltpu.VMEM((ext, k), BF16), pltpu.VMEM((ext, tn), F32), pltpu.VMEM((ext, tn), F32)],
        compiler_params=_params(("parallel", "arbitrary"), 56),
        name="ffn_up_conv_act",
    )(h, h, h, w_up, w_up, conv_w, conv_b2, conv_w, conv_b2)


def _halo_specs(ts, width, seq_len, col):
    r = ts // HALO
    last = seq_len // HALO - 1
    return [
        pl.BlockSpec((None, ts, width), lambda b, i, j: (b, i, col(j))),
        pl.BlockSpec((None, HALO, width), lambda b, i, j: (b, jnp.maximum(i * r - 1, 0), col(j))),
        pl.BlockSpec((None, HALO, width), lambda b, i, j: (b, jnp.minimum((i + 1) * r, last), col(j))),
    ]


def _extended(main_ref, prev_ref, next_ref, i, n_tiles):
    prev = jnp.where(i > 0, prev_ref[...].astype(F32), 0.0)
    nxt = jnp.where(i < n_tiles - 1, next_ref[...].astype(F32), 0.0)
    return jnp.concatenate([prev, main_ref[...].astype(F32), nxt], axis=0)


def _shift_rows(ext, k):
    if k == 0:
        return ext
    n = ext.shape[0]
    return pltpu.roll(ext, (-k) % n, axis=0)


def _centre(ext, ts):
    return ext[HALO:HALO + ts]


def _pool_kernel(m_ref, p_ref, n_ref, w_ref, sc_ref, o_ref, *, ts, seq_len):
    i = pl.program_id(1)
    ext = _extended(m_ref, p_ref, n_ref, i, seq_len // ts)
    pos = i * ts + lax.broadcasted_iota(jnp.int32, (ts, 1), 0)
    for gi, win in enumerate(POOL_WINDOWS):
        sl = slice(gi * POOL_GROUP_DIM, (gi + 1) * POOL_GROUP_DIM)
        e = ext[:, sl]
        acc = _shift_rows(e, -1) + e
        w = 2
        while w < win:
            acc = _shift_rows(acc, -(w // 2)) + _shift_rows(acc, w // 2)
            w *= 2
        lo = jnp.maximum(pos - win // 2, 0)
        hi = jnp.minimum(pos + win - win // 2, seq_len)
        cnt = (hi - lo).astype(F32)
        p = (_centre(acc, ts) / cnt - _centre(e, ts)).astype(BF16)
        y = jnp.dot(p, w_ref[gi], preferred_element_type=F32)
        o_ref[:, sl] = (y * sc_ref[:, sl]).astype(o_ref.dtype)


def _pool_mixer(a_in, pool_w, pool_scale):
    b, l, c = a_in.shape
    ts = 512
    kern = functools.partial(_pool_kernel, ts=ts, seq_len=l)
    return pl.pallas_call(
        kern,
        grid=(b, l // ts, 1),
        in_specs=_halo_specs(ts, c, l, lambda j: 0) + [
            pl.BlockSpec((len(POOL_WINDOWS), POOL_GROUP_DIM, POOL_GROUP_DIM), lambda bi, i, j: (0, 0, 0)),
            pl.BlockSpec((1, c), lambda bi, i, j: (0, 0)),
        ],
        out_specs=pl.BlockSpec((None, ts, c), lambda bi, i, j: (bi, i, 0)),
        out_shape=jax.ShapeDtypeStruct((b, l, c), BF16),
        compiler_params=_params(("parallel", "parallel", "arbitrary"), 48),
        name="pool_mixer",
    )(a_in, a_in, a_in, pool_w, pool_scale.reshape(1, c))


def _dwconv(ext, w_ref, b_ref, ts):
    width = w_ref.shape[0]
    acc = None
    for k in range(width):
        term = _centre(_shift_rows(ext, k - width // 2), ts) * w_ref[k:k + 1, :]
        acc = term if acc is None else acc + term
    return acc + b_ref[...]


def _ssd_conv_kernel(m_ref, p_ref, n_ref, w_ref, b_ref, o_ref, *, ts, seq_len):
    ext = _extended(m_ref, p_ref, n_ref, pl.program_id(1), seq_len // ts)
    o_ref[...] = _silu(_dwconv(ext, w_ref, b_ref, ts)).astype(o_ref.dtype)


def _ssd_conv(xbc, conv_w, conv_b):
    b, l, c = xbc.shape
    ts, tc = 512, 512
    kern = functools.partial(_ssd_conv_kernel, ts=ts, seq_len=l)
    return pl.pallas_call(
        kern,
        grid=(b, l // ts, c // tc),
        in_specs=_halo_specs(ts, tc, l, lambda j: j) + [
            pl.BlockSpec((conv_w.shape[0], tc), lambda bi, i, j: (0, j)),
            pl.BlockSpec((1, tc), lambda bi, i, j: (0, j)),
        ],
        out_specs=pl.BlockSpec((None, ts, tc), lambda bi, i, j: (bi, i, j)),
        out_shape=jax.ShapeDtypeStruct((b, l, c), BF16),
        compiler_params=_params(("parallel", "parallel", "arbitrary"), 32),
        name="ssd_conv",
    )(xbc, xbc, xbc, conv_w, conv_b.reshape(1, c))


def _split_dot(lhs, rhs, terms, split_lhs):
    x = lhs if split_lhs else rhs
    acc = None
    for _ in range(terms):
        hi = x.astype(BF16)
        part = (jnp.dot(hi, rhs, preferred_element_type=F32) if split_lhs
                else jnp.dot(lhs, hi, preferred_element_type=F32))
        acc = part if acc is None else acc + part
        x = x - hi.astype(F32)
    return acc


def _ssd_kernel(xs_ref, bm_ref, cm_ref, dtr_ref, dtb_ref, alog_ref, e_ref, *rest, reverse, final):
    if final:
        yf_ref, z_ref, dsk_ref, ng_ref, o_ref, st_ref = rest
    else:
        o_ref, st_ref = rest
    tc = SSD_CHUNK
    gw = SSD_GROUP_WIDTH

    @pl.when(pl.program_id(1) == 0)
    def _():
        st_ref[...] = jnp.zeros_like(st_ref)

    lane0 = SSD_HEADS if reverse else 0
    dt = _softplus(dtr_ref[...] + dtb_ref[...])
    dta = dt * (-jnp.exp(alog_ref[...]))
    ti = lax.broadcasted_iota(jnp.int32, (tc, tc), 0)
    ui = lax.broadcasted_iota(jnp.int32, (tc, tc), 1)
    causal = (ui >= ti) if reverse else (ui <= ti)
    cs = _split_dot(jnp.where(causal, 1.0, 0.0).astype(BF16), dta, 3, split_lhs=False)
    total = cs[0:1] if reverse else cs[tc - 1:tc]
    w1 = dt * jnp.exp(total - cs)
    chunk_decay = jnp.broadcast_to(jnp.exp(total), (8, LANE))
    cd_e = _split_dot(chunk_decay, e_ref[...], 2, split_lhs=True)[0:1]
    cs2 = cs * LOG2_E
    src2_t = (cs2 - jnp.log2(dt)).T
    w1_t = w1.T
    lane = lax.broadcasted_iota(jnp.int32, (tc, LANE), 1)
    first_head = lane < SSD_HEAD_DIM

    def two_heads(x):
        zero = jnp.zeros_like(x)
        return jnp.where(first_head, x, zero), jnp.where(first_head, zero, x)

    for g in range(SSD_GROUPS):
        gs = slice(g * gw, (g + 1) * gw)
        bm = bm_ref[:, g * SSD_STATE:(g + 1) * SSD_STATE]
        cm = cm_ref[:, g * SSD_STATE:(g + 1) * SSD_STATE]
        cb = lax.dot_general(cm, bm, (((1,), (1,)), ((), ())), preferred_element_type=F32)
        bm_t = bm.astype(F32).T
        cm32 = cm.astype(F32)
        s_prev = st_ref[g]
        s_prev16 = s_prev.astype(BF16)
        parts, states = [], []
        for pr in range(gw // LANE):
            h0 = g * (gw // SSD_HEAD_DIM) + 2 * pr
            lhs_y, lhs_s = [], []
            for h in (h0, h0 + 1):
                hl = slice(lane0 + h, lane0 + h + 1)
                col = jnp.broadcast_to(cs2[:, hl], (tc, tc))
                decay_dt = jnp.exp2(jnp.where(causal, col - src2_t[hl, :], NEG_INF))
                lhs_y.append((cb * decay_dt).astype(BF16))
                lhs_y.append((cm32 * jnp.exp2(col)).astype(BF16))
                lhs_s.append((bm_t * w1_t[lane0 + h:lane0 + h + 1, :]).astype(BF16))
            ps = slice(pr * LANE, (pr + 1) * LANE)
            x_top, x_bot = two_heads(xs_ref[:, h0 * SSD_HEAD_DIM:(h0 + 2) * SSD_HEAD_DIM])
            s_top, s_bot = two_heads(s_prev16[:, ps])
            rhs_y = jnp.concatenate([x_top, s_top, x_bot, s_bot], axis=0)
            parts.append(jnp.dot(jnp.concatenate(lhs_y, axis=1), rhs_y, preferred_element_type=F32))
            states.append(jnp.dot(jnp.concatenate(lhs_s, axis=1), jnp.concatenate([x_top, x_bot], axis=0),
                                  preferred_element_type=F32))
        st_ref[g] = s_prev * cd_e[:, gs] + jnp.concatenate(states, axis=1)
        y = jnp.concatenate(parts, axis=1)
        if final:
            y = y + yf_ref[:, gs] + xs_ref[:, gs].astype(F32) * dsk_ref[:, gs]
            y = y * _silu(z_ref[:, gs].astype(F32))
            y = y * lax.rsqrt(jnp.mean(y * y, axis=-1, keepdims=True) + EPS)
            o_ref[:, gs] = (y * ng_ref[:, gs]).astype(o_ref.dtype)
        else:
            o_ref[:, gs] = y


def _head_expander(reverse):
    e = np.zeros((LANE, SSD_INNER), np.float32)
    lane0 = SSD_HEADS if reverse else 0
    for h in range(SSD_HEADS):
        e[lane0 + h, h * SSD_HEAD_DIM:(h + 1) * SSD_HEAD_DIM] = 1.0
    return jnp.asarray(e, BF16)


def _ssd_pass(xbc_act, dt_raw, dt_bias_row, a_log_row, reverse, final_args=None):
    b, l, _ = xbc_act.shape
    tc = SSD_CHUNK
    nc = l // tc
    final = final_args is not None
    chunk = (lambda c: nc - 1 - c) if reverse else (lambda c: c)
    seq = lambda width, colblk: pl.BlockSpec((None, tc, width), lambda bi, c: (bi, chunk(c), colblk))
    row = lambda width: pl.BlockSpec((1, width), lambda bi, c: (0, 0))
    in_specs = [
        seq(SSD_INNER, 0),
        seq(SSD_BC, SSD_INNER // SSD_BC),
        seq(SSD_BC, SSD_INNER // SSD_BC + 1),
        seq(DT_LANES, 0),
        row(DT_LANES), row(DT_LANES),
        pl.BlockSpec((LANE, SSD_INNER), lambda bi, c: (0, 0)),
    ]
    args = [xbc_act, xbc_act, xbc_act, dt_raw, dt_bias_row, a_log_row, _head_expander(reverse)]
    if final:
        y_fwd, z, d_skip_row, norm_row = final_args
        in_specs += [seq(SSD_INNER, 0), seq(SSD_INNER, 0), row(SSD_INNER), row(SSD_INNER)]
        args += [y_fwd, z, d_skip_row, norm_row]
    return pl.pallas_call(
        functools.partial(_ssd_kernel, reverse=reverse, final=final),
        grid=(b, nc),
        in_specs=in_specs,
        out_specs=seq(SSD_INNER, 0),
        out_shape=jax.ShapeDtypeStruct((b, l, SSD_INNER), BF16 if final else F32),
        scratch_shapes=[pltpu.VMEM((SSD_GROUPS, SSD_STATE, SSD_GROUP_WIDTH), F32)],
        compiler_params=_params(("parallel", "arbitrary"), 32),
        name="ssd_bwd_final" if final else "ssd_fwd",
    )(*args)


def _ssd_mixer(z, xbc, dt_raw, conv_w, conv_b, dt_bias, a_log, d_skip, norm_g):
    xbc_act = _ssd_conv(xbc, conv_w, conv_b)
    pad = DT_LANES - 2 * SSD_HEADS
    dtb = jnp.pad(dt_bias.reshape(1, 2 * SSD_HEADS), ((0, 0), (0, pad)))
    alog = jnp.pad(a_log.reshape(1, 2 * SSD_HEADS), ((0, 0), (0, pad)))
    d_row = jnp.repeat(d_skip, SSD_HEAD_DIM).reshape(1, SSD_INNER)
    y_fwd = _ssd_pass(xbc_act, dt_raw, dtb, alog, reverse=False)
    return _ssd_pass(xbc_act, dt_raw, dtb, alog, reverse=True,
                     final_args=(y_fwd, z, d_row, norm_g.reshape(1, SSD_INNER)))


ATTN_QB = 128
ATTN_KB = ATTN_QB + 2 * ATTN_RADIUS


def _t5_bucket_table():
    half = T5_BUCKETS // 2
    max_exact = half // 2
    delta = np.arange(ATTN_KB)[None, :] - ATTN_RADIUS - np.arange(ATTN_QB)[:, None]
    out = []
    for dil in DILATIONS:
        rel = delta * dil
        n = np.abs(rel)
        large = max_exact + (np.log(np.maximum(n, max_exact) / max_exact) / np.log(T5_MAX_DISTANCE / max_exact)
                             * (half - max_exact)).astype(np.int32)
        large = np.minimum(large, half - 1)
        out.append((rel > 0).astype(np.int32) * half + np.where(n < max_exact, n, large).astype(np.int32))
    return np.stack(out).astype(np.int32)


def _bias_kernel(tab_ref, bkt_ref, o_ref):
    g = pl.program_id(0)
    bk = bkt_ref[...]
    for h in range(ATTN_HEADS):
        acc = jnp.zeros(bk.shape, F32)
        for bucket in range(T5_BUCKETS):
            acc = jnp.where(bk == bucket, tab_ref[bucket, g * ATTN_HEADS + h], acc)
        o_ref[h] = acc


def _t5_bias(t5_table):
    return pl.pallas_call(
        _bias_kernel,
        grid=(ATTN_GROUPS,),
        in_specs=[
            pl.BlockSpec(memory_space=pltpu.SMEM),
            pl.BlockSpec((None, ATTN_QB, ATTN_KB), lambda g: (g, 0, 0)),
        ],
        out_specs=pl.BlockSpec((None, ATTN_HEADS, ATTN_QB, ATTN_KB), lambda g: (g, 0, 0, 0)),
        out_shape=jax.ShapeDtypeStruct((ATTN_GROUPS, ATTN_HEADS, ATTN_QB, ATTN_KB), F32),
        compiler_params=_params(("arbitrary",), 16),
        name="t5_bias",
    )(t5_table, jnp.asarray(_t5_bucket_table()))


ATTN_CHUNK = 1024


_HIGH_HALF = 0xFFFF0000


def _attn_fill(dst, row0, src_ref, col0, n_rows, low_ref=None, low_col0=0):
    step = min(n_rows, LANE)

    def body(c, carry):
        r = pl.multiple_of(c * step, step)
        for h in range(ATTN_HEADS):
            x = src_ref[pl.ds(r, step), col0 + h * ATTN_HEAD_DIM:col0 + (h + 1) * ATTN_HEAD_DIM].astype(F32)
            if low_ref is not None:
                y = low_ref[pl.ds(r, step), low_col0 + h * ATTN_HEAD_DIM:low_col0 + (h + 1) * ATTN_HEAD_DIM]
                x = lax.bitcast_convert_type(x, jnp.uint32) | (lax.bitcast_convert_type(y.astype(F32), jnp.uint32) >> 16)
            dst[h, pl.ds(row0 + r, step), :] = x
        return carry

    lax.fori_loop(0, n_rows // step, body, 0)


def _attn_kernel(q_ref, k_ref, v_ref, *rest, seq_len):
    halos = rest[:4 * ATTN_GROUPS]
    bias_ref, o_ref, qn_s, kv_s, m_s, d_s, n_s = rest[4 * ATTN_GROUPS:]
    i = pl.program_id(1)
    n_chunks = seq_len // ATTN_CHUNK
    r_ = ATTN_RADIUS
    ones = jnp.ones((ATTN_KB, ATTN_HEAD_DIM), BF16)
    tail = kv_s.shape[1] - (ATTN_CHUNK + 2 * r_ * DILATIONS[-1])
    for h in range(ATTN_HEADS):
        kv_s[h, kv_s.shape[1] - tail:, :] = jnp.zeros((tail, ATTN_HEAD_DIM), jnp.uint32)
    for g, dil in enumerate(DILATIONS):
        hb = r_ * dil
        n_strided = ATTN_CHUNK // dil
        qb = min(ATTN_QB, n_strided)
        kb = ATTN_KB
        nblk = n_strided // qb
        kp_ref, kn_ref, vp_ref, vn_ref = halos[4 * g:4 * g + 4]
        col = g * ATTN_OUT
        _attn_fill(qn_s, 0, q_ref, col, ATTN_CHUNK)
        _attn_fill(kv_s, 0, kp_ref, 0, hb, vp_ref, 0)
        _attn_fill(kv_s, hb, k_ref, col, ATTN_CHUNK, v_ref, col)
        _attn_fill(kv_s, hb + ATTN_CHUNK, kn_ref, 0, hb, vn_ref, 0)

        qq = lax.broadcasted_iota(jnp.int32, (qb, kb), 0)
        kk = lax.broadcasted_iota(jnp.int32, (qb, kb), 1)
        band = jnp.abs(kk - r_ - qq) <= r_

        def block(it, carry, g=g, dil=dil, qb=qb, kb=kb, nblk=nblk, band=band, kk=kk):
            res = it // nblk
            blk = it % nblk
            start = res + blk * (qb * dil)
            lo = jnp.where((i == 0) & (blk == 0), r_, 0)
            hi = jnp.where((i == n_chunks - 1) & (blk == nblk - 1), qb + r_, kb)
            valid = band & (kk >= lo) & (kk < hi)
            rows = pl.ds(start, qb, stride=dil)
            heads = range(ATTN_HEADS)
            qs = [qn_s[h, rows, :] for h in heads]
            kvs = [kv_s[h, pl.ds(start, kb, stride=dil), :] for h in heads]
            old = [(m_s[h, rows, :], d_s[h, rows, :], n_s[h, rows, :]) for h in heads] if g > 0 else None
            new = []
            for h in heads:
                k = lax.bitcast_convert_type(kvs[h] & jnp.uint32(_HIGH_HALF), F32).astype(BF16)
                v = lax.bitcast_convert_type(kvs[h] << 16, F32).astype(BF16)
                s = lax.dot_general(qs[h].astype(BF16), k, (((1,), (1,)), ((), ())), preferred_element_type=F32)
                s = jnp.where(valid, s + bias_ref[g, h, :qb, :kb], NEG_INF)
                mx = jnp.max(s, axis=-1, keepdims=True)
                ex = jnp.exp(s - mx).astype(BF16)
                nd = jnp.dot(ex, jnp.concatenate([v, ones[:kb]], axis=1), preferred_element_type=F32)
                num, den = nd[:, :ATTN_HEAD_DIM], nd[:, ATTN_HEAD_DIM:]
                mx = jnp.broadcast_to(mx, (qb, ATTN_HEAD_DIM))
                if g > 0:
                    m_old, d_old, n_old = old[h]
                    m_new = jnp.maximum(m_old, mx)
                    w_old = jnp.exp(m_old - m_new)
                    w_new = jnp.exp(mx - m_new)
                    num = n_old * w_old + num * w_new
                    den = d_old * w_old + den * w_new
                    mx = m_new
                new.append((mx, den, num))
            for h in heads:
                m_s[h, rows, :], d_s[h, rows, :], n_s[h, rows, :] = new[h]
            return carry

        lax.fori_loop(0, dil * nblk, block, 0, unroll=2)

    def finish(c, carry):
        r = pl.multiple_of(c * LANE, LANE)
        for h in range(ATTN_HEADS):
            out = n_s[h, pl.ds(r, LANE), :] / d_s[h, pl.ds(r, LANE), :]
            o_ref[pl.ds(r, LANE), h * ATTN_HEAD_DIM:(h + 1) * ATTN_HEAD_DIM] = out.astype(o_ref.dtype)
        return carry

    lax.fori_loop(0, ATTN_CHUNK // LANE, finish, 0)


def _attention_mixer(qkv, bias):
    b, l, _ = qkv.shape
    c = ATTN_CHUNK
    third = ATTN_QKV // 3
    main = lambda which: pl.BlockSpec((None, c, third), lambda bi, i: (bi, i, which))
    in_specs = [main(0), main(1), main(2)]
    args = [qkv, qkv, qkv]
    for g, dil in enumerate(DILATIONS):
        hb = ATTN_RADIUS * dil
        per = c // hb
        last = l // hb - 1
        mode = dict(pipeline_mode=pl.Buffered(1)) if hb >= c else {}
        for which in (1, 2):
            colblk = which * ATTN_GROUPS + g
            in_specs.append(pl.BlockSpec((None, hb, ATTN_OUT),
                                         lambda bi, i, per=per, colblk=colblk: (bi, jnp.maximum(i * per - 1, 0), colblk),
                                         **mode))
            in_specs.append(pl.BlockSpec((None, hb, ATTN_OUT),
                                         lambda bi, i, per=per, last=last, colblk=colblk:
                                         (bi, jnp.minimum((i + 1) * per, last), colblk), **mode))
            args += [qkv, qkv]
    in_specs.append(pl.BlockSpec((ATTN_GROUPS, ATTN_HEADS, ATTN_QB, ATTN_KB), lambda bi, i: (0, 0, 0, 0),
                                 pipeline_mode=pl.Buffered(1)))
    args.append(bias)
    ext = max(c + 2 * ATTN_RADIUS * DILATIONS[-1], ATTN_KB * DILATIONS[-1])
    head_rows = lambda n: pltpu.VMEM((ATTN_HEADS, n, ATTN_HEAD_DIM), F32)
    return pl.pallas_call(
        functools.partial(_attn_kernel, seq_len=l),
        grid=(b, l // c),
        in_specs=in_specs,
        out_specs=pl.BlockSpec((None, c, ATTN_OUT), lambda bi, i: (bi, i, 0)),
        out_shape=jax.ShapeDtypeStruct((b, l, ATTN_OUT), BF16),
        scratch_shapes=[head_rows(c), pltpu.VMEM((ATTN_HEADS, ext, ATTN_HEAD_DIM), jnp.uint32),
                        head_rows(c), head_rows(c), head_rows(c)],
        compiler_params=_params(("parallel", "arbitrary"), 56),
        name="dilated_attention",
    )(*args)


def _merge_kernel(ya_ref, yb_ref, yc_ref, gl_ref, pa_ref, pb_ref, pc_ref, g0_ref, g1_ref, g2_ref,
                  b0_ref, b1_ref, b2_ref, o_ref):
    gl = gl_ref[...]
    dot = lambda x, w_ref: jnp.dot(x, w_ref[...], preferred_element_type=F32)
    acc = _sigmoid(dot(gl, g0_ref) + b0_ref[...]) * dot(ya_ref[...], pa_ref)
    acc += _sigmoid(dot(gl, g1_ref) + b1_ref[...]) * dot(yb_ref[...], pb_ref)
    acc += _sigmoid(dot(gl, g2_ref) + b2_ref[...]) * dot(yc_ref[...], pc_ref)
    o_ref[...] = acc.astype(o_ref.dtype)


def _gated_merge(y_a, y_b, y_c, g_low, proj_a, proj_b, proj_c, gate_up, gate_b, layer, tm, tn):
    m = y_a.shape[0]
    d = proj_a.shape[2]
    nj = d // tn
    act = lambda k: pl.BlockSpec((tm, k), lambda i, j: (i, 0))
    wt = lambda k, o: _weight_spec(k, tn, layer, lambda j: j + o * nj)
    gate_b2 = gate_b.reshape(DEPTH, 1, 3 * d)
    return pl.pallas_call(
        _merge_kernel,
        grid=(m // tm, nj),
        in_specs=[act(y_a.shape[1]), act(y_b.shape[1]), act(y_c.shape[1]), act(g_low.shape[1]),
                  wt(proj_a.shape[1], 0), wt(proj_b.shape[1], 0), wt(proj_c.shape[1], 0),
                  wt(GATE_RANK, 0), wt(GATE_RANK, 1), wt(GATE_RANK, 2),
                  wt(1, 0), wt(1, 1), wt(1, 2)],
        out_specs=pl.BlockSpec((tm, tn), lambda i, j: (i, j)),
        out_shape=jax.ShapeDtypeStruct((m, d), BF16),
        compiler_params=_params(("parallel", "arbitrary"), 56),
        name="gated_merge",
    )(y_a, y_b, y_c, g_low, proj_a, proj_b, proj_c, gate_up, gate_up, gate_up, gate_b2, gate_b2, gate_b2)


def _reorder_w_in(w_in):
    o = np.cumsum((0, POOL_WIDTH, SSD_INNER, SSD_XBC, 2 * SSD_HEADS, ATTN_QKV, GATE_RANK))
    main = jnp.concatenate([w_in[..., o[0]:o[3]], w_in[..., o[4]:o[6]], w_in[..., o[3]:o[4]]], axis=-1)
    return jnp.pad(main.astype(BF16), ((0, 0), (0, 0), (0, _WIN_TILES * _WIN_TN - main.shape[-1])))


def kernel(x, c, ada_w, ada_b, ada_layer, t5_table, norm_mix, norm_mlp, w_in, pool_w, pool_scale, ssd_conv_w, ssd_conv_b, ssd_dt_bias, ssd_a_log, ssd_d, ssd_norm, q_norm, k_norm, proj_a, proj_b, proj_c, gate_up, gate_b, w_out, mlp_up, mlp_conv_w, mlp_conv_b, mlp_down):
    b, l, d = x.shape
    m = b * l
    mod = _modulation(c, ada_w, ada_b, ada_layer)
    bias = _t5_bias(t5_table)
    w_in_r = _reorder_w_in(w_in)
    pool_w, proj_a, proj_b, proj_c, gate_up, w_out, mlp_up, mlp_down = (
        w.astype(BF16) for w in (pool_w, proj_a, proj_b, proj_c, gate_up, w_out, mlp_up, mlp_down))
    for layer in range(DEPTH):
        shift_m, scale_m, gate_m, shift_f, scale_f, gate_f = (mod[layer, :, k] for k in range(N_MOD))
        h = _norm_mod(x, norm_mix[layer], scale_m, shift_m)
        a_in, z, xbc, qkv, g_low, dt_raw = _in_projection(h.reshape(m, d), w_in_r, layer, q_norm[layer],
                                                          k_norm[layer], 1024)
        seq = lambda t: t.reshape(b, l, t.shape[-1])
        y_a = _pool_mixer(seq(a_in), pool_w[layer], pool_scale[layer])
        y_b = _ssd_mixer(seq(z), seq(xbc), seq(dt_raw), ssd_conv_w[layer], ssd_conv_b[layer], ssd_dt_bias[layer],
                         ssd_a_log[layer], ssd_d[layer], ssd_norm[layer])
        y_c = _attention_mixer(seq(qkv), bias)
        merged = _gated_merge(y_a.reshape(m, -1), y_b.reshape(m, -1), y_c.reshape(m, -1), g_low,
                              proj_a, proj_b, proj_c, gate_up, gate_b, layer, 1024, 512)
        x = _matmul_residual(merged, w_out, layer, x.reshape(m, d), gate_m, 1024, 512, "out_proj").reshape(b, l, d)
        h = _norm_mod(x, norm_mlp[layer], scale_f, shift_f)
        act = _ffn_up_act(h.reshape(m, d), mlp_up, layer, mlp_conv_w[layer], mlp_conv_b[layer], l, 1024, 512)
        x = _matmul_residual(act, mlp_down, layer, x.reshape(m, d), gate_f, 512, 512, "ffn_down").reshape(b, l, d)
    return x
```

```python
import functools

import numpy as np
import jax
import jax.numpy as jnp
from jax import lax
from jax.experimental import pallas as pl
from jax.experimental.pallas import tpu as pltpu

F32 = jnp.float32
BF16 = jnp.bfloat16

D_MODEL = 4096
DEPTH = 4
N_MOD = 6
EPS = 1e-6
NEG_INF = -1e30
LOG2_E = 1.4426950408889634

POOL_WINDOWS = (2, 4, 8, 16)
POOL_WIDTH = 1536
POOL_GROUP_DIM = 384

SSD_INNER = 2048
SSD_HEADS = 32
SSD_HEAD_DIM = 64
SSD_GROUPS = 4
SSD_STATE = 128
SSD_CHUNK = 128
SSD_BC = 512
SSD_XBC = 3072
SSD_GROUP_WIDTH = SSD_INNER // SSD_GROUPS

DILATIONS = (1, 4, 16)
ATTN_RADIUS = 64
ATTN_GROUPS = 3
ATTN_HEADS = 4
ATTN_HEAD_DIM = 128
ATTN_OUT = 512
ATTN_QKV = 4608
T5_BUCKETS = 32
T5_MAX_DISTANCE = 1024
GATE_RANK = 512
MLP_HIDDEN = 8192

HALO = 16
LANE = 128
MIB = 1024 * 1024


def _params(sem, vmem_mib):
    return pltpu.CompilerParams(dimension_semantics=sem, vmem_limit_bytes=vmem_mib * MIB)


def _sigmoid(x):
    return 1.0 / (1.0 + jnp.exp(-x))


def _silu(x):
    return x * _sigmoid(x)


def _softplus(x):
    return jnp.maximum(x, 0.0) + jnp.log1p(jnp.exp(-jnp.abs(x)))


def _mod_kernel(c_ref, w_ref, b_ref, al_ref, o_ref, s_s):
    n_batch = c_ref.shape[0]

    @pl.when(pl.program_id(0) == 0)
    def _():
        s_s[...] = _silu(c_ref[...])

    tn = w_ref.shape[1]
    rows = []
    for bi in range(n_batch):
        s = s_s[bi]
        parts = [jnp.sum(w_ref[:, t * LANE:(t + 1) * LANE] * s, axis=0, keepdims=True) for t in range(tn // LANE)]
        rows.append(jnp.concatenate(parts, axis=1) + b_ref[...])
    base = jnp.concatenate(rows + [jnp.zeros((8 - n_batch, tn), F32)], axis=0)
    o_ref[...] = base[None, :, :] + al_ref[...][:, None, :]


def _modulation(c, ada_w, ada_b, ada_layer):
    b = c.shape[0]
    n = ada_w.shape[1]
    tn = 512
    c_lanes = jnp.broadcast_to(c[:, :, None], (b, D_MODEL, LANE))
    out = pl.pallas_call(
        _mod_kernel,
        grid=(n // tn,),
        in_specs=[
            pl.BlockSpec((b, D_MODEL, LANE), lambda j: (0, 0, 0)),
            pl.BlockSpec((D_MODEL, tn), lambda j: (0, j)),
            pl.BlockSpec((1, tn), lambda j: (0, j)),
            pl.BlockSpec((DEPTH, tn), lambda j: (0, j)),
        ],
        out_specs=pl.BlockSpec((DEPTH, 8, tn), lambda j: (0, 0, j)),
        out_shape=jax.ShapeDtypeStruct((DEPTH, 8, n), F32),
        scratch_shapes=[pltpu.VMEM((b, D_MODEL, LANE), F32)],
        compiler_params=_params(("arbitrary",), 40),
        name="adaln_mod",
    )(c_lanes, ada_w, ada_b.reshape(1, n), ada_layer)
    return out[:, :b].reshape(DEPTH, b, N_MOD, 1, D_MODEL)


def _norm_kernel(x_ref, g_ref, sc_ref, sh_ref, o_ref):
    x = x_ref[...]
    ms = jnp.mean(x * x, axis=-1, keepdims=True)
    y = x * lax.rsqrt(ms + EPS) * g_ref[...]
    o_ref[...] = (y * (1.0 + sc_ref[...]) + sh_ref[...]).astype(o_ref.dtype)


def _norm_mod(x, g, scale, shift):
    b, l, d = x.shape
    ts = 512
    return pl.pallas_call(
        _norm_kernel,
        grid=(b, l // ts),
        in_specs=[
            pl.BlockSpec((None, ts, d), lambda bi, i: (bi, i, 0)),
            pl.BlockSpec((1, d), lambda bi, i: (0, 0)),
            pl.BlockSpec((None, 1, d), lambda bi, i: (bi, 0, 0)),
            pl.BlockSpec((None, 1, d), lambda bi, i: (bi, 0, 0)),
        ],
        out_specs=pl.BlockSpec((None, ts, d), lambda bi, i: (bi, i, 0)),
        out_shape=jax.ShapeDtypeStruct((b, l, d), BF16),
        compiler_params=_params(("parallel", "parallel"), 40),
        name="norm_mod",
    )(x, g.reshape(1, d), scale, shift)


def _weight_spec(k, tn, layer, col=lambda j: j):
    return pl.BlockSpec((None, k, tn), lambda i, j: (layer, 0, col(j)))


def _mm_res_kernel(a_ref, b_ref, x_ref, g_ref, o_ref):
    acc = jnp.dot(a_ref[...], b_ref[...], preferred_element_type=F32)
    o_ref[...] = x_ref[...] + g_ref[...] * acc


def _matmul_residual(a, w, layer, x, gate, tm, tn, name):
    m, k = a.shape
    n = w.shape[2]
    blocks_per_batch = (m // gate.shape[0]) // tm
    return pl.pallas_call(
        _mm_res_kernel,
        grid=(m // tm, n // tn),
        in_specs=[
            pl.BlockSpec((tm, k), lambda i, j: (i, 0)),
            _weight_spec(k, tn, layer),
            pl.BlockSpec((tm, tn), lambda i, j: (i, j)),
            pl.BlockSpec((None, 1, tn), lambda i, j: (i // blocks_per_batch, 0, j)),
        ],
        out_specs=pl.BlockSpec((tm, tn), lambda i, j: (i, j)),
        out_shape=jax.ShapeDtypeStruct((m, n), F32),
        compiler_params=_params(("parallel", "arbitrary"), 56),
        name=name,
    )(a, w, x, gate)


_WIN_TN = 512
_WIN_SEGMENTS = (POOL_WIDTH, SSD_INNER, SSD_XBC, ATTN_QKV, GATE_RANK)
_WIN_TILE_START = tuple(int(s) // _WIN_TN for s in np.cumsum((0,) + _WIN_SEGMENTS))
_WIN_TILES = _WIN_TILE_START[-1] + 1
_WIN_QKV = 3
DT_LANES = LANE


def _head_rmsnorm(x, gain):
    out = []
    for h in range(ATTN_HEADS):
        xh = x[:, h * ATTN_HEAD_DIM:(h + 1) * ATTN_HEAD_DIM]
        ms = jnp.mean(xh * xh, axis=-1, keepdims=True)
        out.append(xh * lax.rsqrt(ms + EPS) * gain)
    return jnp.concatenate(out, axis=1)


def _win_kernel(a_ref, b_ref, qg_ref, kg_ref, *o_refs):
    j = pl.program_id(1)
    q0 = _WIN_TILE_START[_WIN_QKV]
    k0, v0 = q0 + ATTN_GROUPS, q0 + 2 * ATTN_GROUPS

    @pl.when(j == _WIN_TILES - 1)
    def _():
        o_refs[-1][...] = jnp.dot(a_ref[...], b_ref[:, :DT_LANES], preferred_element_type=F32)

    @pl.when(j < _WIN_TILES - 1)
    def _():
        acc = jnp.dot(a_ref[...], b_ref[...], preferred_element_type=F32)
        for s, o_ref in enumerate(o_refs[:-1]):
            lo, hi = (v0, _WIN_TILE_START[s + 1]) if s == _WIN_QKV else (_WIN_TILE_START[s], _WIN_TILE_START[s + 1])

            @pl.when((j >= lo) & (j < hi))
            def _(o_ref=o_ref):
                o_ref[...] = acc.astype(o_ref.dtype)

        @pl.when((j >= q0) & (j < k0))
        def _():
            o_refs[_WIN_QKV][...] = _head_rmsnorm(acc, qg_ref[...] * ATTN_HEAD_DIM ** -0.5).astype(BF16)

        @pl.when((j >= k0) & (j < v0))
        def _():
            o_refs[_WIN_QKV][...] = _head_rmsnorm(acc, kg_ref[...]).astype(BF16)


def _in_projection(h, w_in_r, layer, q_norm, k_norm, tm):
    m, k = h.shape

    def seg_spec(s):
        lo, n_tiles = _WIN_TILE_START[s], _WIN_TILE_START[s + 1] - _WIN_TILE_START[s]
        return pl.BlockSpec((tm, _WIN_TN), lambda i, j: (i, jnp.clip(j - lo, 0, n_tiles - 1)))

    out_specs = [seg_spec(s) for s in range(len(_WIN_SEGMENTS))]
    out_specs.append(pl.BlockSpec((tm, DT_LANES), lambda i, j: (i, 0)))
    out_shape = [jax.ShapeDtypeStruct((m, w), BF16) for w in _WIN_SEGMENTS]
    out_shape.append(jax.ShapeDtypeStruct((m, DT_LANES), F32))
    gain_spec = pl.BlockSpec((1, ATTN_HEAD_DIM), lambda i, j: (0, 0))
    return pl.pallas_call(
        _win_kernel,
        grid=(m // tm, _WIN_TILES),
        in_specs=[
            pl.BlockSpec((tm, k), lambda i, j: (i, 0)),
            _weight_spec(k, _WIN_TN, layer),
            gain_spec, gain_spec,
        ],
        out_specs=out_specs,
        out_shape=out_shape,
        compiler_params=_params(("parallel", "arbitrary"), 56),
        name="in_proj",
    )(h, w_in_r, q_norm.reshape(1, ATTN_HEAD_DIM), k_norm.reshape(1, ATTN_HEAD_DIM))


FFN_CONV = 3


def _ffn_up_kernel(hm_ref, hp_ref, hn_ref, wu_ref, wv_ref, cwu_ref, cbu_ref, cwv_ref, cbv_ref, o_ref,
                   a_s, u_s, v_s, *, tm, blocks_per_seq):
    @pl.when(pl.program_id(1) == 0)
    def _():
        pos = pl.program_id(0) % blocks_per_seq
        a_s[0:HALO] = jnp.where(pos > 0, hp_ref[...], jnp.zeros_like(hp_ref))
        a_s[HALO:HALO + tm] = hm_ref[...]
        a_s[HALO + tm:] = jnp.where(pos < blocks_per_seq - 1, hn_ref[...], jnp.zeros_like(hn_ref))

    a = a_s[...]
    u_s[...] = jnp.dot(a, wu_ref[...], preferred_element_type=F32)
    v_s[...] = jnp.dot(a, wv_ref[...], preferred_element_type=F32)

    def conv(s_ref, w_ref, b_ref):
        acc = b_ref[...]
        for k in range(FFN_CONV):
            r0 = HALO + k - FFN_CONV // 2
            acc = acc + s_ref[r0:r0 + tm, :] * w_ref[k:k + 1, :]
        return acc

    o_ref[...] = (_silu(conv(u_s, cwu_ref, cbu_ref)) * conv(v_s, cwv_ref, cbv_ref)).astype(o_ref.dtype)


def _ffn_up_act(h, w_up, layer, conv_w, conv_b, seq_len, tm, tn):
    m, k = h.shape
    hid = w_up.shape[2] // 2
    off = hid // tn
    r = tm // HALO
    last = m // HALO - 1
    conv_b2 = conv_b.reshape(1, 2 * hid)
    cw = lambda o: pl.BlockSpec((FFN_CONV, tn), lambda i, j: (0, j + o))
    cb = lambda o: pl.BlockSpec((1, tn), lambda i, j: (0, j + o))
    ext = tm + 2 * HALO
    return pl.pallas_call(
        functools.partial(_ffn_up_kernel, tm=tm, blocks_per_seq=seq_len // tm),
        grid=(m // tm, hid // tn),
        in_specs=[
            pl.BlockSpec((tm, k), lambda i, j: (i, 0), pipeline_mode=pl.Buffered(1)),
            pl.BlockSpec((HALO, k), lambda i, j: (jnp.maximum(i * r - 1, 0), 0)),
            pl.BlockSpec((HALO, k), lambda i, j: (jnp.minimum((i + 1) * r, last), 0)),
            _weight_spec(k, tn, layer), _weight_spec(k, tn, layer, lambda j: j + off),
            cw(0), cb(0), cw(off), cb(off),
        ],
        out_specs=pl.BlockSpec((tm, tn), lambda i, j: (i, j)),
        out_shape=jax.ShapeDtypeStruct((m, hid), BF16),
        scratch_shapes=[pltpu.VMEM((ext, k), BF16), pltpu.VMEM((ext, tn), F32), pltpu.VMEM((ext, tn), F32)],
        compiler_params=_params(("parallel", "arbitrary"), 56),
        name="ffn_up_conv_act",
    )(h, h, h, w_up, w_up, conv_w, conv_b2, conv_w, conv_b2)


def _halo_specs(ts, width, seq_len, col):
    r = ts // HALO
    last = seq_len // HALO - 1
    return [
        pl.BlockSpec((None, ts, width), lambda b, i, j: (b, i, col(j))),
        pl.BlockSpec((None, HALO, width), lambda b, i, j: (b, jnp.maximum(i * r - 1, 0), col(j))),
        pl.BlockSpec((None, HALO, width), lambda b, i, j: (b, jnp.minimum((i + 1) * r, last), col(j))),
    ]


ROW_BLOCK = 128
ROW_WINDOW = ROW_BLOCK + 2 * HALO


def _fill_extended(ext_s, main_ref, prev_ref, next_ref, i, n_tiles):
    ts = main_ref.shape[0]
    ext_s[0:HALO] = jnp.where(i > 0, prev_ref[...], jnp.zeros_like(prev_ref))
    ext_s[HALO:HALO + ts] = main_ref[...]
    ext_s[HALO + ts:] = jnp.where(i < n_tiles - 1, next_ref[...], jnp.zeros_like(next_ref))


def _row_mixers(offset_ranges):
    t = lax.broadcasted_iota(jnp.int32, (ROW_BLOCK, ROW_WINDOW), 0)
    j = lax.broadcasted_iota(jnp.int32, (ROW_BLOCK, ROW_WINDOW), 1) - HALO
    mats = [jnp.where((j >= t + lo) & (j <= t + hi), 1.0, 0.0).astype(BF16) for lo, hi in offset_ranges]
    return jnp.concatenate(mats, axis=0)


def _shift_rows(ext, k):
    n = ext.shape[0]
    return pltpu.roll(ext, (-k) % n, axis=0)


def _pool_kernel(m_ref, p_ref, n_ref, w_ref, sc_ref, o_ref, *, ts, seq_len):
    i = pl.program_id(1)
    prev = jnp.where(i > 0, p_ref[...].astype(F32), 0.0)
    nxt = jnp.where(i < seq_len // ts - 1, n_ref[...].astype(F32), 0.0)
    ext = jnp.concatenate([prev, m_ref[...].astype(F32), nxt], axis=0)
    centre = lambda t: t[HALO:HALO + ts]
    pos = i * ts + lax.broadcasted_iota(jnp.int32, (ts, 1), 0)
    for gi, win in enumerate(POOL_WINDOWS):
        sl = slice(gi * POOL_GROUP_DIM, (gi + 1) * POOL_GROUP_DIM)
        e = ext[:, sl]
        acc = _shift_rows(e, -1) + e
        w = 2
        while w < win:
            acc = _shift_rows(acc, -(w // 2)) + _shift_rows(acc, w // 2)
            w *= 2
        lo = jnp.maximum(pos - win // 2, 0)
        hi = jnp.minimum(pos + win - win // 2, seq_len)
        cnt = (hi - lo).astype(F32)
        p = (centre(acc) / cnt - centre(e)).astype(BF16)
        y = jnp.dot(p, w_ref[gi], preferred_element_type=F32)
        o_ref[:, sl] = (y * sc_ref[:, sl]).astype(o_ref.dtype)


def _pool_mixer(a_in, pool_w, pool_scale):
    b, l, c = a_in.shape
    ts = 512
    kern = functools.partial(_pool_kernel, ts=ts, seq_len=l)
    return pl.pallas_call(
        kern,
        grid=(b, l // ts, 1),
        in_specs=_halo_specs(ts, c, l, lambda j: 0) + [
            pl.BlockSpec((len(POOL_WINDOWS), POOL_GROUP_DIM, POOL_GROUP_DIM), lambda bi, i, j: (0, 0, 0)),
            pl.BlockSpec((1, c), lambda bi, i, j: (0, 0)),
        ],
        out_specs=pl.BlockSpec((None, ts, c), lambda bi, i, j: (bi, i, 0)),
        out_shape=jax.ShapeDtypeStruct((b, l, c), BF16),
        compiler_params=_params(("parallel", "parallel", "arbitrary"), 48),
        name="pool_mixer",
    )(a_in, a_in, a_in, pool_w, pool_scale.reshape(1, c))


def _ssd_conv_kernel(m_ref, p_ref, n_ref, w_ref, b_ref, o_ref, ext_s, *, ts, seq_len):
    _fill_extended(ext_s, m_ref, p_ref, n_ref, pl.program_id(1), seq_len // ts)
    width = w_ref.shape[0]
    taps = [k - width // 2 for k in range(width)]
    shifts = _row_mixers([(d, d) for d in taps if d != 0])
    for blk in range(ts // ROW_BLOCK):
        r0 = blk * ROW_BLOCK
        moved = jnp.dot(shifts, ext_s[r0:r0 + ROW_WINDOW, :], preferred_element_type=F32)
        acc = b_ref[...]
        m = 0
        for k, d in enumerate(taps):
            if d == 0:
                xk = ext_s[HALO + r0:HALO + r0 + ROW_BLOCK, :].astype(F32)
            else:
                xk = moved[m * ROW_BLOCK:(m + 1) * ROW_BLOCK]
                m += 1
            acc = acc + xk * w_ref[k:k + 1, :]
        o_ref[r0:r0 + ROW_BLOCK, :] = _silu(acc).astype(o_ref.dtype)


def _ssd_conv(xbc, conv_w, conv_b):
    b, l, c = xbc.shape
    ts, tc = 512, 512
    kern = functools.partial(_ssd_conv_kernel, ts=ts, seq_len=l)
    return pl.pallas_call(
        kern,
        grid=(b, l // ts, c // tc),
        in_specs=_halo_specs(ts, tc, l, lambda j: j) + [
            pl.BlockSpec((conv_w.shape[0], tc), lambda bi, i, j: (0, j)),
            pl.BlockSpec((1, tc), lambda bi, i, j: (0, j)),
        ],
        out_specs=pl.BlockSpec((None, ts, tc), lambda bi, i, j: (bi, i, j)),
        out_shape=jax.ShapeDtypeStruct((b, l, c), BF16),
        scratch_shapes=[pltpu.VMEM((ts + 2 * HALO, tc), BF16)],
        compiler_params=_params(("parallel", "parallel", "arbitrary"), 32),
        name="ssd_conv",
    )(xbc, xbc, xbc, conv_w, conv_b.reshape(1, c))


def _split_dot(lhs, rhs, terms, split_lhs):
    x = lhs if split_lhs else rhs
    acc = None
    for _ in range(terms):
        hi = x.astype(BF16)
        part = (jnp.dot(hi, rhs, preferred_element_type=F32) if split_lhs
                else jnp.dot(lhs, hi, preferred_element_type=F32))
        acc = part if acc is None else acc + part
        x = x - hi.astype(F32)
    return acc


def _ssd_kernel(xs_ref, bm_ref, cm_ref, dtr_ref, dtb_ref, alog_ref, e_ref, *rest, reverse, final):
    if final:
        yf_ref, z_ref, dsk_ref, ng_ref, o_ref, st_ref = rest
    else:
        o_ref, st_ref = rest
    tc = SSD_CHUNK
    gw = SSD_GROUP_WIDTH

    @pl.when(pl.program_id(1) == 0)
    def _():
        st_ref[...] = jnp.zeros_like(st_ref)

    lane0 = SSD_HEADS if reverse else 0
    dt = _softplus(dtr_ref[...] + dtb_ref[...])
    dta = dt * (-jnp.exp(alog_ref[...]))
    ti = lax.broadcasted_iota(jnp.int32, (tc, tc), 0)
    ui = lax.broadcasted_iota(jnp.int32, (tc, tc), 1)
    causal = (ui >= ti) if reverse else (ui <= ti)
    cs = _split_dot(jnp.where(causal, 1.0, 0.0).astype(BF16), dta, 3, split_lhs=False)
    total = cs[0:1] if reverse else cs[tc - 1:tc]
    w1 = dt * jnp.exp(total - cs)
    chunk_decay = jnp.broadcast_to(jnp.exp(total), (8, LANE))
    cd_e = _split_dot(chunk_decay, e_ref[...], 2, split_lhs=True)[0:1]
    cs2 = cs * LOG2_E
    src2_t = (cs2 - jnp.log2(dt)).T
    w1_t = w1.T
    lane = lax.broadcasted_iota(jnp.int32, (tc, LANE), 1)
    first_head = lane < SSD_HEAD_DIM

    def two_heads(x):
        zero = jnp.zeros_like(x)
        return jnp.where(first_head, x, zero), jnp.where(first_head, zero, x)

    for g in range(SSD_GROUPS):
        gs = slice(g * gw, (g + 1) * gw)
        bm = bm_ref[:, g * SSD_STATE:(g + 1) * SSD_STATE]
        cm = cm_ref[:, g * SSD_STATE:(g + 1) * SSD_STATE]
        cb = lax.dot_general(cm, bm, (((1,), (1,)), ((), ())), preferred_element_type=F32)
        bm_t = bm.astype(F32).T
        cm32 = cm.astype(F32)
        s_prev = st_ref[g]
        s_prev16 = s_prev.astype(BF16)
        parts, states = [], []
        for pr in range(gw // LANE):
            h0 = g * (gw // SSD_HEAD_DIM) + 2 * pr
            lhs_y, lhs_s = [], []
            for h in (h0, h0 + 1):
                hl = slice(lane0 + h, lane0 + h + 1)
                col = jnp.broadcast_to(cs2[:, hl], (tc, tc))
                decay_dt = jnp.exp2(jnp.where(causal, col - src2_t[hl, :], NEG_INF))
                lhs_y.append((cb * decay_dt).astype(BF16))
                lhs_y.append((cm32 * jnp.exp2(col)).astype(BF16))
                lhs_s.append((bm_t * w1_t[lane0 + h:lane0 + h + 1, :]).astype(BF16))
            ps = slice(pr * LANE, (pr + 1) * LANE)
            x_top, x_bot = two_heads(xs_ref[:, h0 * SSD_HEAD_DIM:(h0 + 2) * SSD_HEAD_DIM])
            s_top, s_bot = two_heads(s_prev16[:, ps])
            rhs_y = jnp.concatenate([x_top, s_top, x_bot, s_bot], axis=0)
            parts.append(jnp.dot(jnp.concatenate(lhs_y, axis=1), rhs_y, preferred_element_type=F32))
            states.append(jnp.dot(jnp.concatenate(lhs_s, axis=1), jnp.concatenate([x_top, x_bot], axis=0),
                                  preferred_element_type=F32))
        st_ref[g] = s_prev * cd_e[:, gs] + jnp.concatenate(states, axis=1)
        y = jnp.concatenate(parts, axis=1)
        if final:
            y = y + yf_ref[:, gs] + xs_ref[:, gs].astype(F32) * dsk_ref[:, gs]
            y = y * _silu(z_ref[:, gs].astype(F32))
            y = y * lax.rsqrt(jnp.mean(y * y, axis=-1, keepdims=True) + EPS)
            o_ref[:, gs] = (y * ng_ref[:, gs]).astype(o_ref.dtype)
        else:
            o_ref[:, gs] = y


def _head_expander(reverse):
    e = np.zeros((LANE, SSD_INNER), np.float32)
    lane0 = SSD_HEADS if reverse else 0
    for h in range(SSD_HEADS):
        e[lane0 + h, h * SSD_HEAD_DIM:(h + 1) * SSD_HEAD_DIM] = 1.0
    return jnp.asarray(e, BF16)


def _ssd_pass(xbc_act, dt_raw, dt_bias_row, a_log_row, reverse, final_args=None):
    b, l, _ = xbc_act.shape
    tc = SSD_CHUNK
    nc = l // tc
    final = final_args is not None
    chunk = (lambda c: nc - 1 - c) if reverse else (lambda c: c)
    seq = lambda width, colblk: pl.BlockSpec((None, tc, width), lambda bi, c: (bi, chunk(c), colblk))
    row = lambda width: pl.BlockSpec((1, width), lambda bi, c: (0, 0))
    in_specs = [
        seq(SSD_INNER, 0),
        seq(SSD_BC, SSD_INNER // SSD_BC),
        seq(SSD_BC, SSD_INNER // SSD_BC + 1),
        seq(DT_LANES, 0),
        row(DT_LANES), row(DT_LANES),
        pl.BlockSpec((LANE, SSD_INNER), lambda bi, c: (0, 0)),
    ]
    args = [xbc_act, xbc_act, xbc_act, dt_raw, dt_bias_row, a_log_row, _head_expander(reverse)]
    if final:
        y_fwd, z, d_skip_row, norm_row = final_args
        in_specs += [seq(SSD_INNER, 0), seq(SSD_INNER, 0), row(SSD_INNER), row(SSD_INNER)]
        args += [y_fwd, z, d_skip_row, norm_row]
    return pl.pallas_call(
        functools.partial(_ssd_kernel, reverse=reverse, final=final),
        grid=(b, nc),
        in_specs=in_specs,
        out_specs=seq(SSD_INNER, 0),
        out_shape=jax.ShapeDtypeStruct((b, l, SSD_INNER), BF16 if final else F32),
        scratch_shapes=[pltpu.VMEM((SSD_GROUPS, SSD_STATE, SSD_GROUP_WIDTH), F32)],
        compiler_params=_params(("parallel", "arbitrary"), 32),
        name="ssd_bwd_final" if final else "ssd_fwd",
    )(*args)


def _ssd_mixer(z, xbc, dt_raw, conv_w, conv_b, dt_bias, a_log, d_skip, norm_g):
    xbc_act = _ssd_conv(xbc, conv_w, conv_b)
    pad = DT_LANES - 2 * SSD_HEADS
    dtb = jnp.pad(dt_bias.reshape(1, 2 * SSD_HEADS), ((0, 0), (0, pad)))
    alog = jnp.pad(a_log.reshape(1, 2 * SSD_HEADS), ((0, 0), (0, pad)))
    d_row = jnp.repeat(d_skip, SSD_HEAD_DIM).reshape(1, SSD_INNER)
    y_fwd = _ssd_pass(xbc_act, dt_raw, dtb, alog, reverse=False)
    return _ssd_pass(xbc_act, dt_raw, dtb, alog, reverse=True,
                     final_args=(y_fwd, z, d_row, norm_g.reshape(1, SSD_INNER)))


ATTN_QB = 128
ATTN_KB = ATTN_QB + 2 * ATTN_RADIUS


def _t5_bucket_table():
    half = T5_BUCKETS // 2
    max_exact = half // 2
    delta = np.arange(ATTN_KB)[None, :] - ATTN_RADIUS - np.arange(ATTN_QB)[:, None]
    out = []
    for dil in DILATIONS:
        rel = delta * dil
        n = np.abs(rel)
        large = max_exact + (np.log(np.maximum(n, max_exact) / max_exact) / np.log(T5_MAX_DISTANCE / max_exact)
                             * (half - max_exact)).astype(np.int32)
        large = np.minimum(large, half - 1)
        out.append((rel > 0).astype(np.int32) * half + np.where(n < max_exact, n, large).astype(np.int32))
    return np.stack(out).astype(np.int32)


def _bias_kernel(tab_ref, bkt_ref, o_ref):
    g = pl.program_id(0)
    bk = bkt_ref[...]
    for h in range(ATTN_HEADS):
        acc = jnp.zeros(bk.shape, F32)
        for bucket in range(T5_BUCKETS):
            acc = jnp.where(bk == bucket, tab_ref[bucket, g * ATTN_HEADS + h], acc)
        o_ref[h] = acc


def _t5_bias(t5_table):
    return pl.pallas_call(
        _bias_kernel,
        grid=(ATTN_GROUPS,),
        in_specs=[
            pl.BlockSpec(memory_space=pltpu.SMEM),
            pl.BlockSpec((None, ATTN_QB, ATTN_KB), lambda g: (g, 0, 0)),
        ],
        out_specs=pl.BlockSpec((None, ATTN_HEADS, ATTN_QB, ATTN_KB), lambda g: (g, 0, 0, 0)),
        out_shape=jax.ShapeDtypeStruct((ATTN_GROUPS, ATTN_HEADS, ATTN_QB, ATTN_KB), F32),
        compiler_params=_params(("arbitrary",), 16),
        name="t5_bias",
    )(t5_table, jnp.asarray(_t5_bucket_table()))


ATTN_CHUNK = 1024


_HIGH_HALF = 0xFFFF0000


def _attn_fill(dst, row0, src_ref, col0, n_rows, low_ref=None, low_col0=0):
    step = min(n_rows, LANE)

    def body(c, carry):
        r = pl.multiple_of(c * step, step)
        for h in range(ATTN_HEADS):
            x = src_ref[pl.ds(r, step), col0 + h * ATTN_HEAD_DIM:col0 + (h + 1) * ATTN_HEAD_DIM].astype(F32)
            if low_ref is not None:
                y = low_ref[pl.ds(r, step), low_col0 + h * ATTN_HEAD_DIM:low_col0 + (h + 1) * ATTN_HEAD_DIM]
                x = lax.bitcast_convert_type(x, jnp.uint32) | (lax.bitcast_convert_type(y.astype(F32), jnp.uint32) >> 16)
            dst[h, pl.ds(row0 + r, step), :] = x
        return carry

    lax.fori_loop(0, n_rows // step, body, 0)


def _attn_kernel(q_ref, k_ref, v_ref, *rest, seq_len):
    halos = rest[:4 * ATTN_GROUPS]
    bias_ref, o_ref, qn_s, kv_s, m_s, d_s, n_s = rest[4 * ATTN_GROUPS:]
    i = pl.program_id(1)
    n_chunks = seq_len // ATTN_CHUNK
    r_ = ATTN_RADIUS
    ones = jnp.ones((ATTN_KB, ATTN_HEAD_DIM), BF16)
    tail = kv_s.shape[1] - (ATTN_CHUNK + 2 * r_ * DILATIONS[-1])
    for h in range(ATTN_HEADS):
        kv_s[h, kv_s.shape[1] - tail:, :] = jnp.zeros((tail, ATTN_HEAD_DIM), jnp.uint32)
    for g, dil in enumerate(DILATIONS):
        hb = r_ * dil
        n_strided = ATTN_CHUNK // dil
        qb = min(ATTN_QB, n_strided)
        kb = ATTN_KB
        nblk = n_strided // qb
        kp_ref, kn_ref, vp_ref, vn_ref = halos[4 * g:4 * g + 4]
        col = g * ATTN_OUT
        _attn_fill(qn_s, 0, q_ref, col, ATTN_CHUNK)
        _attn_fill(kv_s, 0, kp_ref, 0, hb, vp_ref, 0)
        _attn_fill(kv_s, hb, k_ref, col, ATTN_CHUNK, v_ref, col)
        _attn_fill(kv_s, hb + ATTN_CHUNK, kn_ref, 0, hb, vn_ref, 0)

        qq = lax.broadcasted_iota(jnp.int32, (qb, kb), 0)
        kk = lax.broadcasted_iota(jnp.int32, (qb, kb), 1)
        band = jnp.abs(kk - r_ - qq) <= r_

        def block(it, carry, g=g, dil=dil, qb=qb, kb=kb, nblk=nblk, band=band, kk=kk):
            res = it // nblk
            blk = it % nblk
            start = res + blk * (qb * dil)
            lo = jnp.where((i == 0) & (blk == 0), r_, 0)
            hi = jnp.where((i == n_chunks - 1) & (blk == nblk - 1), qb + r_, kb)
            valid = band & (kk >= lo) & (kk < hi)
            rows = pl.ds(start, qb, stride=dil)
            heads = range(ATTN_HEADS)
            qs = [qn_s[h, rows, :] for h in heads]
            kvs = [kv_s[h, pl.ds(start, kb, stride=dil), :] for h in heads]
            old = [(m_s[h, rows, :], d_s[h, rows, :], n_s[h, rows, :]) for h in heads] if g > 0 else None
            new = []
            for h in heads:
                k = lax.bitcast_convert_type(kvs[h] & jnp.uint32(_HIGH_HALF), F32).astype(BF16)
                v = lax.bitcast_convert_type(kvs[h] << 16, F32).astype(BF16)
                s = lax.dot_general(qs[h].astype(BF16), k, (((1,), (1,)), ((), ())), preferred_element_type=F32)
                s = jnp.where(valid, s + bias_ref[g, h, :qb, :kb], NEG_INF)
                mx = jnp.max(s, axis=-1, keepdims=True)
                ex = jnp.exp(s - mx).astype(BF16)
                nd = jnp.dot(ex, jnp.concatenate([v, ones[:kb]], axis=1), preferred_element_type=F32)
                num, den = nd[:, :ATTN_HEAD_DIM], nd[:, ATTN_HEAD_DIM:]
                mx = jnp.broadcast_to(mx, (qb, ATTN_HEAD_DIM))
                if g > 0:
                    m_old, d_old, n_old = old[h]
                    m_new = jnp.maximum(m_old, mx)
                    w_old = jnp.exp(m_old - m_new)
                    w_new = jnp.exp(mx - m_new)
                    num = n_old * w_old + num * w_new
                    den = d_old * w_old + den * w_new
                    mx = m_new
                new.append((mx, den, num))
            for h in heads:
                m_s[h, rows, :], d_s[h, rows, :], n_s[h, rows, :] = new[h]
            return carry

        lax.fori_loop(0, dil * nblk, block, 0, unroll=2 if g == 1 else 4)

    def finish(c, carry):
        r = pl.multiple_of(c * LANE, LANE)
        for h in range(ATTN_HEADS):
            out = n_s[h, pl.ds(r, LANE), :] / d_s[h, pl.ds(r, LANE), :]
            o_ref[pl.ds(r, LANE), h * ATTN_HEAD_DIM:(h + 1) * ATTN_HEAD_DIM] = out.astype(o_ref.dtype)
        return carry

    lax.fori_loop(0, ATTN_CHUNK // LANE, finish, 0)


def _attention_mixer(qkv, bias):
    b, l, _ = qkv.shape
    c = ATTN_CHUNK
    third = ATTN_QKV // 3
    main = lambda which: pl.BlockSpec((None, c, third), lambda bi, i: (bi, i, which))
    in_specs = [main(0), main(1), main(2)]
    args = [qkv, qkv, qkv]
    for g, dil in enumerate(DILATIONS):
        hb = ATTN_RADIUS * dil
        per = c // hb
        last = l // hb - 1
        mode = dict(pipeline_mode=pl.Buffered(1)) if hb >= c else {}
        for which in (1, 2):
            colblk = which * ATTN_GROUPS + g
            in_specs.append(pl.BlockSpec((None, hb, ATTN_OUT),
                                         lambda bi, i, per=per, colblk=colblk: (bi, jnp.maximum(i * per - 1, 0), colblk),
                                         **mode))
            in_specs.append(pl.BlockSpec((None, hb, ATTN_OUT),
                                         lambda bi, i, per=per, last=last, colblk=colblk:
                                         (bi, jnp.minimum((i + 1) * per, last), colblk), **mode))
            args += [qkv, qkv]
    in_specs.append(pl.BlockSpec((ATTN_GROUPS, ATTN_HEADS, ATTN_QB, ATTN_KB), lambda bi, i: (0, 0, 0, 0),
                                 pipeline_mode=pl.Buffered(1)))
    args.append(bias)
    ext = max(c + 2 * ATTN_RADIUS * DILATIONS[-1], ATTN_KB * DILATIONS[-1])
    head_rows = lambda n: pltpu.VMEM((ATTN_HEADS, n, ATTN_HEAD_DIM), F32)
    return pl.pallas_call(
        functools.partial(_attn_kernel, seq_len=l),
        grid=(b, l // c),
        in_specs=in_specs,
        out_specs=pl.BlockSpec((None, c, ATTN_OUT), lambda bi, i: (bi, i, 0)),
        out_shape=jax.ShapeDtypeStruct((b, l, ATTN_OUT), BF16),
        scratch_shapes=[head_rows(c), pltpu.VMEM((ATTN_HEADS, ext, ATTN_HEAD_DIM), jnp.uint32),
                        head_rows(c), head_rows(c), head_rows(c)],
        compiler_params=_params(("parallel", "arbitrary"), 56),
        name="dilated_attention",
    )(*args)


def _merge_kernel(ya_ref, yb_ref, yc_ref, gl_ref, pa_ref, pb_ref, pc_ref, g0_ref, g1_ref, g2_ref,
                  b0_ref, b1_ref, b2_ref, o_ref):
    gl = gl_ref[...]
    dot = lambda x, w_ref: jnp.dot(x, w_ref[...], preferred_element_type=F32)
    acc = _sigmoid(dot(gl, g0_ref) + b0_ref[...]) * dot(ya_ref[...], pa_ref)
    acc += _sigmoid(dot(gl, g1_ref) + b1_ref[...]) * dot(yb_ref[...], pb_ref)
    acc += _sigmoid(dot(gl, g2_ref) + b2_ref[...]) * dot(yc_ref[...], pc_ref)
    o_ref[...] = acc.astype(o_ref.dtype)


def _gated_merge(y_a, y_b, y_c, g_low, proj_a, proj_b, proj_c, gate_up, gate_b, layer, tm, tn):
    m = y_a.shape[0]
    d = proj_a.shape[2]
    nj = d // tn
    act = lambda k: pl.BlockSpec((tm, k), lambda i, j: (i, 0))
    wt = lambda k, o: _weight_spec(k, tn, layer, lambda j: j + o * nj)
    gate_b2 = gate_b.reshape(DEPTH, 1, 3 * d)
    return pl.pallas_call(
        _merge_kernel,
        grid=(m // tm, nj),
        in_specs=[act(y_a.shape[1]), act(y_b.shape[1]), act(y_c.shape[1]), act(g_low.shape[1]),
                  wt(proj_a.shape[1], 0), wt(proj_b.shape[1], 0), wt(proj_c.shape[1], 0),
                  wt(GATE_RANK, 0), wt(GATE_RANK, 1), wt(GATE_RANK, 2),
                  wt(1, 0), wt(1, 1), wt(1, 2)],
        out_specs=pl.BlockSpec((tm, tn), lambda i, j: (i, j)),
        out_shape=jax.ShapeDtypeStruct((m, d), BF16),
        compiler_params=_params(("parallel", "arbitrary"), 56),
        name="gated_merge",
    )(y_a, y_b, y_c, g_low, proj_a, proj_b, proj_c, gate_up, gate_up, gate_up, gate_b2, gate_b2, gate_b2)


def _reorder_w_in(w_in):
    o = np.cumsum((0, POOL_WIDTH, SSD_INNER, SSD_XBC, 2 * SSD_HEADS, ATTN_QKV, GATE_RANK))
    main = jnp.concatenate([w_in[..., o[0]:o[3]], w_in[..., o[4]:o[6]], w_in[..., o[3]:o[4]]], axis=-1)
    return jnp.pad(main.astype(BF16), ((0, 0), (0, 0), (0, _WIN_TILES * _WIN_TN - main.shape[-1])))


def kernel(x, c, ada_w, ada_b, ada_layer, t5_table, norm_mix, norm_mlp, w_in, pool_w, pool_scale, ssd_conv_w, ssd_conv_b, ssd_dt_bias, ssd_a_log, ssd_d, ssd_norm, q_norm, k_norm, proj_a, proj_b, proj_c, gate_up, gate_b, w_out, mlp_up, mlp_conv_w, mlp_conv_b, mlp_down):
    b, l, d = x.shape
    m = b * l
    mod = _modulation(c, ada_w, ada_b, ada_layer)
    bias = _t5_bias(t5_table)
    w_in_r = _reorder_w_in(w_in)
    pool_w, proj_a, proj_b, proj_c, gate_up, w_out, mlp_up, mlp_down = (
        w.astype(BF16) for w in (pool_w, proj_a, proj_b, proj_c, gate_up, w_out, mlp_up, mlp_down))
    for layer in range(DEPTH):
        shift_m, scale_m, gate_m, shift_f, scale_f, gate_f = (mod[layer, :, k] for k in range(N_MOD))
        h = _norm_mod(x, norm_mix[layer], scale_m, shift_m)
        a_in, z, xbc, qkv, g_low, dt_raw = _in_projection(h.reshape(m, d), w_in_r, layer, q_norm[layer],
                                                          k_norm[layer], 1024)
        seq = lambda t: t.reshape(b, l, t.shape[-1])
        y_a = _pool_mixer(seq(a_in), pool_w[layer], pool_scale[layer])
        y_b = _ssd_mixer(seq(z), seq(xbc), seq(dt_raw), ssd_conv_w[layer], ssd_conv_b[layer], ssd_dt_bias[layer],
                         ssd_a_log[layer], ssd_d[layer], ssd_norm[layer])
        y_c = _attention_mixer(seq(qkv), bias)
        merged = _gated_merge(y_a.reshape(m, -1), y_b.reshape(m, -1), y_c.reshape(m, -1), g_low,
                              proj_a, proj_b, proj_c, gate_up, gate_b, layer, 1024, 512)
        x = _matmul_residual(merged, w_out, layer, x.reshape(m, d), gate_m, 1024, 512, "out_proj").reshape(b, l, d)
        h = _norm_mod(x, norm_mlp[layer], scale_f, shift_f)
        act = _ffn_up_act(h.reshape(m, d), mlp_up, layer, mlp_conv_w[layer], mlp_conv_b[layer], l, 1024, 512)
        x = _matmul_residual(act, mlp_down, layer, x.reshape(m, d), gate_f, 512, 512, "ffn_down").reshape(b, l, d)
    return x
```

```python
import functools

import numpy as np
import jax
import jax.numpy as jnp
from jax import lax
from jax.experimental import pallas as pl
from jax.experimental.pallas import tpu as pltpu

F32 = jnp.float32
BF16 = jnp.bfloat16

D_MODEL = 4096
DEPTH = 4
N_MOD = 6
EPS = 1e-6
NEG_INF = -1e30
LOG2_E = 1.4426950408889634

POOL_WINDOWS = (2, 4, 8, 16)
POOL_WIDTH = 1536
POOL_GROUP_DIM = 384

SSD_INNER = 2048
SSD_HEADS = 32
SSD_HEAD_DIM = 64
SSD_GROUPS = 4
SSD_STATE = 128
SSD_CHUNK = 128
SSD_BC = 512
SSD_XBC = 3072
SSD_GROUP_WIDTH = SSD_INNER // SSD_GROUPS

DILATIONS = (1, 4, 16)
ATTN_RADIUS = 64
ATTN_GROUPS = 3
ATTN_HEADS = 4
ATTN_HEAD_DIM = 128
ATTN_OUT = 512
ATTN_QKV = 4608
T5_BUCKETS = 32
T5_MAX_DISTANCE = 1024
GATE_RANK = 512
MLP_HIDDEN = 8192

HALO = 16
LANE = 128
MIB = 1024 * 1024


def _params(sem, vmem_mib):
    return pltpu.CompilerParams(dimension_semantics=sem, vmem_limit_bytes=vmem_mib * MIB)


def _sigmoid(x):
    return 1.0 / (1.0 + jnp.exp(-x))


def _silu(x):
    return x * _sigmoid(x)


def _softplus(x):
    return jnp.maximum(x, 0.0) + jnp.log1p(jnp.exp(-jnp.abs(x)))


def _mod_kernel(c_ref, w_ref, b_ref, al_ref, o_ref):
    n_batch = c_ref.shape[0]
    tn = w_ref.shape[1]

    @pl.when(pl.program_id(1) == 0)
    def _():
        o_ref[...] = jnp.broadcast_to((b_ref[...] + al_ref[...])[:, None, :], o_ref.shape)

    rows = []
    for bi in range(n_batch):
        s = _silu(c_ref[bi])
        parts = [jnp.sum(w_ref[:, t * LANE:(t + 1) * LANE] * s, axis=0, keepdims=True) for t in range(tn // LANE)]
        rows.append(jnp.concatenate(parts, axis=1))
    part = jnp.concatenate(rows + [jnp.zeros((8 - n_batch, tn), F32)], axis=0)
    o_ref[...] += part[None, :, :]


def _modulation(c, ada_w, ada_b, ada_layer):
    b = c.shape[0]
    n = ada_w.shape[1]
    tk, tn = 1024, 2048
    c_lanes = jnp.broadcast_to(c[:, :, None], (b, D_MODEL, LANE))
    out = pl.pallas_call(
        _mod_kernel,
        grid=(n // tn, D_MODEL // tk),
        in_specs=[
            pl.BlockSpec((b, tk, LANE), lambda j, k: (0, k, 0)),
            pl.BlockSpec((tk, tn), lambda j, k: (k, j)),
            pl.BlockSpec((1, tn), lambda j, k: (0, j)),
            pl.BlockSpec((DEPTH, tn), lambda j, k: (0, j)),
        ],
        out_specs=pl.BlockSpec((DEPTH, 8, tn), lambda j, k: (0, 0, j)),
        out_shape=jax.ShapeDtypeStruct((DEPTH, 8, n), F32),
        compiler_params=_params(("parallel", "arbitrary"), 40),
        name="adaln_mod",
    )(c_lanes, ada_w, ada_b.reshape(1, n), ada_layer)
    return out[:, :b].reshape(DEPTH, b, N_MOD, 1, D_MODEL)


def _norm_kernel(x_ref, g_ref, sc_ref, sh_ref, o_ref):
    x = x_ref[...]
    ms = jnp.mean(x * x, axis=-1, keepdims=True)
    y = x * lax.rsqrt(ms + EPS) * g_ref[...]
    o_ref[...] = (y * (1.0 + sc_ref[...]) + sh_ref[...]).astype(o_ref.dtype)


def _norm_mod(x, g, scale, shift):
    b, l, d = x.shape
    ts = 512
    return pl.pallas_call(
        _norm_kernel,
        grid=(b, l // ts),
        in_specs=[
            pl.BlockSpec((None, ts, d), lambda bi, i: (bi, i, 0)),
            pl.BlockSpec((1, d), lambda bi, i: (0, 0)),
            pl.BlockSpec((None, 1, d), lambda bi, i: (bi, 0, 0)),
            pl.BlockSpec((None, 1, d), lambda bi, i: (bi, 0, 0)),
        ],
        out_specs=pl.BlockSpec((None, ts, d), lambda bi, i: (bi, i, 0)),
        out_shape=jax.ShapeDtypeStruct((b, l, d), BF16),
        compiler_params=_params(("parallel", "parallel"), 40),
        name="norm_mod",
    )(x, g.reshape(1, d), scale, shift)


def _weight_spec(k, tn, layer, col=lambda j: j):
    return pl.BlockSpec((None, k, tn), lambda i, j: (layer, 0, col(j)))


def _mm_res_kernel(a_ref, b_ref, x_ref, g_ref, o_ref):
    acc = jnp.dot(a_ref[...], b_ref[...], preferred_element_type=F32)
    o_ref[...] = x_ref[...] + g_ref[...] * acc


def _matmul_residual(a, w, layer, x, gate, tm, tn, name):
    m, k = a.shape
    n = w.shape[2]
    blocks_per_batch = (m // gate.shape[0]) // tm
    return pl.pallas_call(
        _mm_res_kernel,
        grid=(m // tm, n // tn),
        in_specs=[
            pl.BlockSpec((tm, k), lambda i, j: (i, 0)),
            _weight_spec(k, tn, layer),
            pl.BlockSpec((tm, tn), lambda i, j: (i, j)),
            pl.BlockSpec((None, 1, tn), lambda i, j: (i // blocks_per_batch, 0, j)),
        ],
        out_specs=pl.BlockSpec((tm, tn), lambda i, j: (i, j)),
        out_shape=jax.ShapeDtypeStruct((m, n), F32),
        compiler_params=_params(("parallel", "arbitrary"), 56),
        name=name,
    )(a, w, x, gate)


_WIN_TN = 512
_WIN_SEGMENTS = (ATTN_QKV, POOL_WIDTH, SSD_INNER, SSD_XBC, GATE_RANK)
_WIN_TILE_START = tuple(int(s) // _WIN_TN for s in np.cumsum((0,) + _WIN_SEGMENTS))
_WIN_TILES = _WIN_TILE_START[-1] + 1
_WIN_QKV = 0
DT_LANES = LANE


def _head_rmsnorm(x, gain):
    out = []
    for h in range(ATTN_HEADS):
        xh = x[:, h * ATTN_HEAD_DIM:(h + 1) * ATTN_HEAD_DIM]
        ms = jnp.mean(xh * xh, axis=-1, keepdims=True)
        out.append(xh * lax.rsqrt(ms + EPS) * gain)
    return jnp.concatenate(out, axis=1)


_WIN_Q0 = _WIN_TILE_START[_WIN_QKV]
_WIN_V0 = _WIN_Q0 + 2 * ATTN_GROUPS
_, COL_A_IN, COL_Z, COL_XBC, COL_G_LOW = _WIN_TILE_START[:5]


def _win_kernel(a_ref, b_ref, gain_ref, o_ref, dt_ref):
    j = pl.program_id(1)
    normed = (j >= _WIN_Q0) & (j < _WIN_V0)
    last = j == _WIN_TILES - 1

    @pl.when(last)
    def _():
        dt_ref[...] = jnp.dot(a_ref[...], b_ref[:, :DT_LANES], preferred_element_type=F32)

    @pl.when(normed)
    def _():
        acc = jnp.dot(a_ref[...], b_ref[...], preferred_element_type=F32)
        o_ref[...] = _head_rmsnorm(acc, gain_ref[...]).astype(o_ref.dtype)

    @pl.when(jnp.logical_not(normed | last))
    def _():
        o_ref[...] = jnp.dot(a_ref[...], b_ref[...], preferred_element_type=F32).astype(o_ref.dtype)


def _in_projection(h, w_in_r, layer, q_norm, k_norm, tm):
    m, k = h.shape
    n_main = (_WIN_TILES - 1) * _WIN_TN
    gains = jnp.stack([q_norm * ATTN_HEAD_DIM ** -0.5, k_norm]).reshape(2, 1, ATTN_HEAD_DIM)
    return pl.pallas_call(
        _win_kernel,
        grid=(m // tm, _WIN_TILES),
        in_specs=[
            pl.BlockSpec((tm, k), lambda i, j: (i, 0)),
            _weight_spec(k, _WIN_TN, layer),
            pl.BlockSpec((None, 1, ATTN_HEAD_DIM), lambda i, j: (jnp.clip((j - _WIN_Q0) // ATTN_GROUPS, 0, 1), 0, 0)),
        ],
        out_specs=[pl.BlockSpec((tm, _WIN_TN), lambda i, j: (i, jnp.minimum(j, _WIN_TILES - 2))),
                   pl.BlockSpec((tm, DT_LANES), lambda i, j: (i, 0))],
        out_shape=[jax.ShapeDtypeStruct((m, n_main), BF16), jax.ShapeDtypeStruct((m, DT_LANES), F32)],
        compiler_params=_params(("parallel", "arbitrary"), 56),
        name="in_proj",
    )(h, w_in_r, gains)


FFN_CONV = 3


def _ffn_up_kernel(hm_ref, hp_ref, hn_ref, wu_ref, wv_ref, cwu_ref, cbu_ref, cwv_ref, cbv_ref, o_ref,
                   a_s, u_s, v_s, *, tm, blocks_per_seq):
    @pl.when(pl.program_id(1) == 0)
    def _():
        pos = pl.program_id(0) % blocks_per_seq
        a_s[0:HALO] = jnp.where(pos > 0, hp_ref[...], jnp.zeros_like(hp_ref))
        a_s[HALO:HALO + tm] = hm_ref[...]
        a_s[HALO + tm:] = jnp.where(pos < blocks_per_seq - 1, hn_ref[...], jnp.zeros_like(hn_ref))

    a = a_s[...]
    u_s[...] = jnp.dot(a, wu_ref[...], preferred_element_type=F32)
    v_s[...] = jnp.dot(a, wv_ref[...], preferred_element_type=F32)

    def conv(s_ref, w_ref, b_ref):
        acc = b_ref[...]
        for k in range(FFN_CONV):
            r0 = HALO + k - FFN_CONV // 2
            acc = acc + s_ref[r0:r0 + tm, :] * w_ref[k:k + 1, :]
        return acc

    o_ref[...] = (_silu(conv(u_s, cwu_ref, cbu_ref)) * conv(v_s, cwv_ref, cbv_ref)).astype(o_ref.dtype)


def _ffn_up_act(h, w_up, layer, conv_w, conv_b, seq_len, tm, tn):
    m, k = h.shape
    hid = w_up.shape[2] // 2
    off = hid // tn
    r = tm // HALO
    last = m // HALO - 1
    conv_b2 = conv_b.reshape(1, 2 * hid)
    cw = lambda o: pl.BlockSpec((FFN_CONV, tn), lambda i, j: (0, j + o))
    cb = lambda o: pl.BlockSpec((1, tn), lambda i, j: (0, j + o))
    ext = tm + 2 * HALO
    return pl.pallas_call(
        functools.partial(_ffn_up_kernel, tm=tm, blocks_per_seq=seq_len // tm),
        grid=(m // tm, hid // tn),
        in_specs=[
            pl.BlockSpec((tm, k), lambda i, j: (i, 0), pipeline_mode=pl.Buffered(1)),
            pl.BlockSpec((HALO, k), lambda i, j: (jnp.maximum(i * r - 1, 0), 0)),
            pl.BlockSpec((HALO, k), lambda i, j: (jnp.minimum((i + 1) * r, last), 0)),
            _weight_spec(k, tn, layer), _weight_spec(k, tn, layer, lambda j: j + off),
            cw(0), cb(0), cw(off), cb(off),
        ],
        out_specs=pl.BlockSpec((tm, tn), lambda i, j: (i, j)),
        out_shape=jax.ShapeDtypeStruct((m, hid), BF16),
        scratch_shapes=[pltpu.VMEM((ext, k), BF16), pltpu.VMEM((ext, tn), F32), pltpu.VMEM((ext, tn), F32)],
        compiler_params=_params(("parallel", "arbitrary"), 56),
        name="ffn_up_conv_act",
    )(h, h, h, w_up, w_up, conv_w, conv_b2, conv_w, conv_b2)


def _halo_specs(ts, width, seq_len, col):
    r = ts // HALO
    last = seq_len // HALO - 1
    return [
        pl.BlockSpec((None, ts, width), lambda b, i, j: (b, i, col(j))),
        pl.BlockSpec((None, HALO, width), lambda b, i, j: (b, jnp.maximum(i * r - 1, 0), col(j))),
        pl.BlockSpec((None, HALO, width), lambda b, i, j: (b, jnp.minimum((i + 1) * r, last), col(j))),
    ]


ROW_BLOCK = 128
ROW_WINDOW = ROW_BLOCK + 2 * HALO


def _fill_extended(ext_s, main_ref, prev_ref, next_ref, i, n_tiles):
    ts = main_ref.shape[0]
    ext_s[0:HALO] = jnp.where(i > 0, prev_ref[...], jnp.zeros_like(prev_ref))
    ext_s[HALO:HALO + ts] = main_ref[...]
    ext_s[HALO + ts:] = jnp.where(i < n_tiles - 1, next_ref[...], jnp.zeros_like(next_ref))


def _row_mixers(offset_ranges):
    t = lax.broadcasted_iota(jnp.int32, (ROW_BLOCK, ROW_WINDOW), 0)
    j = lax.broadcasted_iota(jnp.int32, (ROW_BLOCK, ROW_WINDOW), 1) - HALO
    mats = [jnp.where((j >= t + lo) & (j <= t + hi), 1.0, 0.0).astype(BF16) for lo, hi in offset_ranges]
    return jnp.concatenate(mats, axis=0)


def _shift_rows(ext, k):
    n = ext.shape[0]
    return pltpu.roll(ext, (-k) % n, axis=0)


def _pool_kernel(m_ref, p_ref, n_ref, w_ref, sc_ref, o_ref, *, ts, seq_len):
    i = pl.program_id(1)
    prev = jnp.where(i > 0, p_ref[...].astype(F32), 0.0)
    nxt = jnp.where(i < seq_len // ts - 1, n_ref[...].astype(F32), 0.0)
    ext = jnp.concatenate([prev, m_ref[...].astype(F32), nxt], axis=0)
    centre = lambda t: t[HALO:HALO + ts]
    pos = i * ts + lax.broadcasted_iota(jnp.int32, (ts, 1), 0)
    for gi, win in enumerate(POOL_WINDOWS):
        sl = slice(gi * POOL_GROUP_DIM, (gi + 1) * POOL_GROUP_DIM)
        e = ext[:, sl]
        acc = _shift_rows(e, -1) + e
        w = 2
        while w < win:
            acc = _shift_rows(acc, -(w // 2)) + _shift_rows(acc, w // 2)
            w *= 2
        lo = jnp.maximum(pos - win // 2, 0)
        hi = jnp.minimum(pos + win - win // 2, seq_len)
        cnt = (hi - lo).astype(F32)
        p = (centre(acc) / cnt - centre(e)).astype(BF16)
        y = jnp.dot(p, w_ref[gi], preferred_element_type=F32)
        o_ref[:, sl] = (y * sc_ref[:, sl]).astype(o_ref.dtype)


def _pool_mixer(a_in, pool_w, pool_scale, col_block=0):
    b, l, _ = a_in.shape
    c = POOL_WIDTH
    ts = 512
    kern = functools.partial(_pool_kernel, ts=ts, seq_len=l)
    return pl.pallas_call(
        kern,
        grid=(b, l // ts, 1),
        in_specs=_halo_specs(ts, c, l, lambda j: col_block) + [
            pl.BlockSpec((len(POOL_WINDOWS), POOL_GROUP_DIM, POOL_GROUP_DIM), lambda bi, i, j: (0, 0, 0)),
            pl.BlockSpec((1, c), lambda bi, i, j: (0, 0)),
        ],
        out_specs=pl.BlockSpec((None, ts, c), lambda bi, i, j: (bi, i, 0)),
        out_shape=jax.ShapeDtypeStruct((b, l, c), BF16),
        compiler_params=_params(("parallel", "parallel", "arbitrary"), 48),
        name="pool_mixer",
    )(a_in, a_in, a_in, pool_w, pool_scale.reshape(1, c))


def _ssd_conv_kernel(m_ref, p_ref, n_ref, w_ref, b_ref, o_ref, ext_s, *, ts, seq_len):
    _fill_extended(ext_s, m_ref, p_ref, n_ref, pl.program_id(1), seq_len // ts)
    width = w_ref.shape[0]
    taps = [k - width // 2 for k in range(width)]
    shifts = _row_mixers([(d, d) for d in taps if d != 0])
    for blk in range(ts // ROW_BLOCK):
        r0 = blk * ROW_BLOCK
        moved = jnp.dot(shifts, ext_s[r0:r0 + ROW_WINDOW, :], preferred_element_type=F32)
        acc = b_ref[...]
        m = 0
        for k, d in enumerate(taps):
            if d == 0:
                xk = ext_s[HALO + r0:HALO + r0 + ROW_BLOCK, :].astype(F32)
            else:
                xk = moved[m * ROW_BLOCK:(m + 1) * ROW_BLOCK]
                m += 1
            acc = acc + xk * w_ref[k:k + 1, :]
        o_ref[r0:r0 + ROW_BLOCK, :] = _silu(acc).astype(o_ref.dtype)


def _ssd_conv(xbc, conv_w, conv_b, col0=0):
    b, l, _ = xbc.shape
    c = SSD_XBC
    ts, tc = 512, 512
    kern = functools.partial(_ssd_conv_kernel, ts=ts, seq_len=l)
    return pl.pallas_call(
        kern,
        grid=(b, l // ts, c // tc),
        in_specs=_halo_specs(ts, tc, l, lambda j: j + col0) + [
            pl.BlockSpec((conv_w.shape[0], tc), lambda bi, i, j: (0, j)),
            pl.BlockSpec((1, tc), lambda bi, i, j: (0, j)),
        ],
        out_specs=pl.BlockSpec((None, ts, tc), lambda bi, i, j: (bi, i, j)),
        out_shape=jax.ShapeDtypeStruct((b, l, c), BF16),
        scratch_shapes=[pltpu.VMEM((ts + 2 * HALO, tc), BF16)],
        compiler_params=_params(("parallel", "parallel", "arbitrary"), 32),
        name="ssd_conv",
    )(xbc, xbc, xbc, conv_w, conv_b.reshape(1, c))


def _split_dot(lhs, rhs, terms, split_lhs):
    x = lhs if split_lhs else rhs
    acc = None
    for _ in range(terms):
        hi = x.astype(BF16)
        part = (jnp.dot(hi, rhs, preferred_element_type=F32) if split_lhs
                else jnp.dot(lhs, hi, preferred_element_type=F32))
        acc = part if acc is None else acc + part
        x = x - hi.astype(F32)
    return acc


def _ssd_kernel(xs_ref, bm_ref, cm_ref, dtr_ref, dtb_ref, alog_ref, e_ref, *rest, reverse, final):
    if final:
        yf_ref, *z_refs, dsk_ref, ng_ref, o_ref, st_ref = rest
    else:
        o_ref, st_ref = rest
    tc = SSD_CHUNK
    gw = SSD_GROUP_WIDTH

    @pl.when(pl.program_id(1) == 0)
    def _():
        st_ref[...] = jnp.zeros_like(st_ref)

    lane0 = SSD_HEADS if reverse else 0
    dt = _softplus(dtr_ref[...] + dtb_ref[...])
    dta = dt * (-jnp.exp(alog_ref[...]))
    ti = lax.broadcasted_iota(jnp.int32, (tc, tc), 0)
    ui = lax.broadcasted_iota(jnp.int32, (tc, tc), 1)
    causal = (ui >= ti) if reverse else (ui <= ti)
    cs = _split_dot(jnp.where(causal, 1.0, 0.0).astype(BF16), dta, 3, split_lhs=False)
    total = cs[0:1] if reverse else cs[tc - 1:tc]
    w1 = dt * jnp.exp(total - cs)
    chunk_decay = jnp.broadcast_to(jnp.exp(total), (8, LANE))
    cd_e = _split_dot(chunk_decay, e_ref[...], 2, split_lhs=True)[0:1]
    cs2 = cs * LOG2_E
    src2_t = (cs2 - jnp.log2(dt)).T
    w1_t = w1.T
    lane = lax.broadcasted_iota(jnp.int32, (tc, LANE), 1)
    first_head = lane < SSD_HEAD_DIM

    def two_heads(x):
        zero = jnp.zeros_like(x)
        return jnp.where(first_head, x, zero), jnp.where(first_head, zero, x)

    for g in range(SSD_GROUPS):
        gs = slice(g * gw, (g + 1) * gw)
        bm = bm_ref[:, g * SSD_STATE:(g + 1) * SSD_STATE]
        cm = cm_ref[:, g * SSD_STATE:(g + 1) * SSD_STATE]
        cb = lax.dot_general(cm, bm, (((1,), (1,)), ((), ())), preferred_element_type=F32)
        bm_t = bm.astype(F32).T
        cm32 = cm.astype(F32)
        s_prev = st_ref[g]
        s_prev16 = s_prev.astype(BF16)
        parts, states = [], []
        for pr in range(gw // LANE):
            h0 = g * (gw // SSD_HEAD_DIM) + 2 * pr
            lhs_y, lhs_s = [], []
            for h in (h0, h0 + 1):
                hl = slice(lane0 + h, lane0 + h + 1)
                col = jnp.broadcast_to(cs2[:, hl], (tc, tc))
                decay_dt = jnp.exp2(jnp.where(causal, col - src2_t[hl, :], NEG_INF))
                lhs_y.append((cb * decay_dt).astype(BF16))
                lhs_y.append((cm32 * jnp.exp2(col)).astype(BF16))
                lhs_s.append((bm_t * w1_t[lane0 + h:lane0 + h + 1, :]).astype(BF16))
            ps = slice(pr * LANE, (pr + 1) * LANE)
            x_top, x_bot = two_heads(xs_ref[:, h0 * SSD_HEAD_DIM:(h0 + 2) * SSD_HEAD_DIM])
            s_top, s_bot = two_heads(s_prev16[:, ps])
            rhs_y = jnp.concatenate([x_top, s_top, x_bot, s_bot], axis=0)
            parts.append(jnp.dot(jnp.concatenate(lhs_y, axis=1), rhs_y, preferred_element_type=F32))
            states.append(jnp.dot(jnp.concatenate(lhs_s, axis=1), jnp.concatenate([x_top, x_bot], axis=0),
                                  preferred_element_type=F32))
        st_ref[g] = s_prev * cd_e[:, gs] + jnp.concatenate(states, axis=1)
        y = jnp.concatenate(parts, axis=1)
        if final:
            y = y + yf_ref[:, gs] + xs_ref[:, gs].astype(F32) * dsk_ref[:, gs]
            y = y * _silu(z_refs[g][...].astype(F32))
            y = y * lax.rsqrt(jnp.mean(y * y, axis=-1, keepdims=True) + EPS)
            o_ref[:, gs] = (y * ng_ref[:, gs]).astype(o_ref.dtype)
        else:
            o_ref[:, gs] = y


def _head_expander(reverse):
    e = np.zeros((LANE, SSD_INNER), np.float32)
    lane0 = SSD_HEADS if reverse else 0
    for h in range(SSD_HEADS):
        e[lane0 + h, h * SSD_HEAD_DIM:(h + 1) * SSD_HEAD_DIM] = 1.0
    return jnp.asarray(e, BF16)


def _ssd_pass(xbc_act, dt_raw, dt_bias_row, a_log_row, reverse, final_args=None):
    b, l, _ = xbc_act.shape
    tc = SSD_CHUNK
    nc = l // tc
    final = final_args is not None
    chunk = (lambda c: nc - 1 - c) if reverse else (lambda c: c)
    seq = lambda width, colblk: pl.BlockSpec((None, tc, width), lambda bi, c: (bi, chunk(c), colblk))
    row = lambda width: pl.BlockSpec((1, width), lambda bi, c: (0, 0))
    in_specs = [
        seq(SSD_INNER, 0),
        seq(SSD_BC, SSD_INNER // SSD_BC),
        seq(SSD_BC, SSD_INNER // SSD_BC + 1),
        seq(DT_LANES, 0),
        row(DT_LANES), row(DT_LANES),
        pl.BlockSpec((LANE, SSD_INNER), lambda bi, c: (0, 0)),
    ]
    args = [xbc_act, xbc_act, xbc_act, dt_raw, dt_bias_row, a_log_row, _head_expander(reverse)]
    if final:
        y_fwd, z, z_col0, d_skip_row, norm_row = final_args
        in_specs += [seq(SSD_INNER, 0)] + [seq(SSD_GROUP_WIDTH, z_col0 + g) for g in range(SSD_GROUPS)]
        in_specs += [row(SSD_INNER), row(SSD_INNER)]
        args += [y_fwd] + [z] * SSD_GROUPS + [d_skip_row, norm_row]
    return pl.pallas_call(
        functools.partial(_ssd_kernel, reverse=reverse, final=final),
        grid=(b, nc),
        in_specs=in_specs,
        out_specs=seq(SSD_INNER, 0),
        out_shape=jax.ShapeDtypeStruct((b, l, SSD_INNER), BF16 if final else F32),
        scratch_shapes=[pltpu.VMEM((SSD_GROUPS, SSD_STATE, SSD_GROUP_WIDTH), F32)],
        compiler_params=_params(("parallel", "arbitrary"), 32),
        name="ssd_bwd_final" if final else "ssd_fwd",
    )(*args)


def _ssd_mixer(proj, dt_raw, conv_w, conv_b, dt_bias, a_log, d_skip, norm_g, z_col0=COL_Z, xbc_col0=COL_XBC):
    xbc_act = _ssd_conv(proj, conv_w, conv_b, xbc_col0)
    pad = DT_LANES - 2 * SSD_HEADS
    dtb = jnp.pad(dt_bias.reshape(1, 2 * SSD_HEADS), ((0, 0), (0, pad)))
    alog = jnp.pad(a_log.reshape(1, 2 * SSD_HEADS), ((0, 0), (0, pad)))
    d_row = jnp.repeat(d_skip, SSD_HEAD_DIM).reshape(1, SSD_INNER)
    y_fwd = _ssd_pass(xbc_act, dt_raw, dtb, alog, reverse=False)
    return _ssd_pass(xbc_act, dt_raw, dtb, alog, reverse=True,
                     final_args=(y_fwd, proj, z_col0, d_row, norm_g.reshape(1, SSD_INNER)))


ATTN_QB = 128
ATTN_KB = ATTN_QB + 2 * ATTN_RADIUS


def _t5_bucket_table():
    half = T5_BUCKETS // 2
    max_exact = half // 2
    delta = np.arange(ATTN_KB)[None, :] - ATTN_RADIUS - np.arange(ATTN_QB)[:, None]
    out = []
    for dil in DILATIONS:
        rel = delta * dil
        n = np.abs(rel)
        large = max_exact + (np.log(np.maximum(n, max_exact) / max_exact) / np.log(T5_MAX_DISTANCE / max_exact)
                             * (half - max_exact)).astype(np.int32)
        large = np.minimum(large, half - 1)
        out.append((rel > 0).astype(np.int32) * half + np.where(n < max_exact, n, large).astype(np.int32))
    return np.stack(out).astype(np.int32)


def _bias_kernel(tab_ref, bkt_ref, o_ref):
    g = pl.program_id(0)
    bk = bkt_ref[...]
    for h in range(ATTN_HEADS):
        acc = jnp.zeros(bk.shape, F32)
        for bucket in range(T5_BUCKETS):
            acc = jnp.where(bk == bucket, tab_ref[bucket, g * ATTN_HEADS + h], acc)
        o_ref[h] = acc


def _t5_bias(t5_table):
    return pl.pallas_call(
        _bias_kernel,
        grid=(ATTN_GROUPS,),
        in_specs=[
            pl.BlockSpec(memory_space=pltpu.SMEM),
            pl.BlockSpec((None, ATTN_QB, ATTN_KB), lambda g: (g, 0, 0)),
        ],
        out_specs=pl.BlockSpec((None, ATTN_HEADS, ATTN_QB, ATTN_KB), lambda g: (g, 0, 0, 0)),
        out_shape=jax.ShapeDtypeStruct((ATTN_GROUPS, ATTN_HEADS, ATTN_QB, ATTN_KB), F32),
        compiler_params=_params(("arbitrary",), 16),
        name="t5_bias",
    )(t5_table, jnp.asarray(_t5_bucket_table()))


ATTN_CHUNK = 1024


_HIGH_HALF = 0xFFFF0000


def _attn_fill(dst, row0, src_ref, col0, n_rows, low_ref=None, low_col0=0):
    step = min(n_rows, LANE)

    def body(c, carry):
        r = pl.multiple_of(c * step, step)
        for h in range(ATTN_HEADS):
            x = src_ref[pl.ds(r, step), col0 + h * ATTN_HEAD_DIM:col0 + (h + 1) * ATTN_HEAD_DIM].astype(F32)
            if low_ref is not None:
                y = low_ref[pl.ds(r, step), low_col0 + h * ATTN_HEAD_DIM:low_col0 + (h + 1) * ATTN_HEAD_DIM]
                x = lax.bitcast_convert_type(x, jnp.uint32) | (lax.bitcast_convert_type(y.astype(F32), jnp.uint32) >> 16)
            dst[h, pl.ds(row0 + r, step), :] = x
        return carry

    lax.fori_loop(0, n_rows // step, body, 0)


def _attn_kernel(q_ref, k_ref, v_ref, *rest, seq_len):
    halos = rest[:4 * ATTN_GROUPS]
    bias_ref, o_ref, qn_s, kv_s, m_s, d_s, n_s = rest[4 * ATTN_GROUPS:]
    i = pl.program_id(1)
    n_chunks = seq_len // ATTN_CHUNK
    r_ = ATTN_RADIUS
    ones = jnp.ones((ATTN_KB, ATTN_HEAD_DIM), BF16)
    tail = kv_s.shape[1] - (ATTN_CHUNK + 2 * r_ * DILATIONS[-1])
    for h in range(ATTN_HEADS):
        kv_s[h, kv_s.shape[1] - tail:, :] = jnp.zeros((tail, ATTN_HEAD_DIM), jnp.uint32)
    for g, dil in enumerate(DILATIONS):
        hb = r_ * dil
        n_strided = ATTN_CHUNK // dil
        qb = min(ATTN_QB, n_strided)
        kb = ATTN_KB
        nblk = n_strided // qb
        kp_ref, kn_ref, vp_ref, vn_ref = halos[4 * g:4 * g + 4]
        col = g * ATTN_OUT
        _attn_fill(qn_s, 0, q_ref, col, ATTN_CHUNK)
        _attn_fill(kv_s, 0, kp_ref, 0, hb, vp_ref, 0)
        _attn_fill(kv_s, hb, k_ref, col, ATTN_CHUNK, v_ref, col)
        _attn_fill(kv_s, hb + ATTN_CHUNK, kn_ref, 0, hb, vn_ref, 0)

        qq = lax.broadcasted_iota(jnp.int32, (qb, kb), 0)
        kk = lax.broadcasted_iota(jnp.int32, (qb, kb), 1)
        band = jnp.abs(kk - r_ - qq) <= r_

        def block(it, carry, g=g, dil=dil, qb=qb, kb=kb, nblk=nblk, band=band, kk=kk):
            res = it // nblk
            blk = it % nblk
            start = res + blk * (qb * dil)
            lo = jnp.where((i == 0) & (blk == 0), r_, 0)
            hi = jnp.where((i == n_chunks - 1) & (blk == nblk - 1), qb + r_, kb)
            valid = band & (kk >= lo) & (kk < hi)
            rows = pl.ds(start, qb, stride=dil)
            heads = range(ATTN_HEADS)
            qs = [qn_s[h, rows, :] for h in heads]
            kvs = [kv_s[h, pl.ds(start, kb, stride=dil), :] for h in heads]
            old = [(m_s[h, rows, :], d_s[h, rows, :], n_s[h, rows, :]) for h in heads] if g > 0 else None
            new = []
            for h in heads:
                k = lax.bitcast_convert_type(kvs[h] & jnp.uint32(_HIGH_HALF), F32).astype(BF16)
                v = lax.bitcast_convert_type(kvs[h] << 16, F32).astype(BF16)
                s = lax.dot_general(qs[h].astype(BF16), k, (((1,), (1,)), ((), ())), preferred_element_type=F32)
                s = jnp.where(valid, s + bias_ref[g, h, :qb, :kb], NEG_INF)
                mx = jnp.max(s, axis=-1, keepdims=True)
                ex = jnp.exp(s - mx).astype(BF16)
                nd = jnp.dot(ex, jnp.concatenate([v, ones[:kb]], axis=1), preferred_element_type=F32)
                num, den = nd[:, :ATTN_HEAD_DIM], nd[:, ATTN_HEAD_DIM:]
                mx = jnp.broadcast_to(mx, (qb, ATTN_HEAD_DIM))
                if g > 0:
                    m_old, d_old, n_old = old[h]
                    m_new = jnp.maximum(m_old, mx)
                    w_old = jnp.exp(m_old - m_new)
                    w_new = jnp.exp(mx - m_new)
                    num = n_old * w_old + num * w_new
                    den = d_old * w_old + den * w_new
                    mx = m_new
                new.append((mx, den, num))
            for h in heads:
                m_s[h, rows, :], d_s[h, rows, :], n_s[h, rows, :] = new[h]
            return carry

        lax.fori_loop(0, dil * nblk, block, 0, unroll=2 if g == 1 else 4)

    def finish(c, carry):
        r = pl.multiple_of(c * LANE, LANE)
        for h in range(ATTN_HEADS):
            out = n_s[h, pl.ds(r, LANE), :] / d_s[h, pl.ds(r, LANE), :]
            o_ref[pl.ds(r, LANE), h * ATTN_HEAD_DIM:(h + 1) * ATTN_HEAD_DIM] = out.astype(o_ref.dtype)
        return carry

    lax.fori_loop(0, ATTN_CHUNK // LANE, finish, 0)


def _attention_mixer(qkv, bias):
    b, l, _ = qkv.shape
    c = ATTN_CHUNK
    third = ATTN_QKV // 3
    main = lambda which: pl.BlockSpec((None, c, third), lambda bi, i: (bi, i, which))
    in_specs = [main(0), main(1), main(2)]
    args = [qkv, qkv, qkv]
    for g, dil in enumerate(DILATIONS):
        hb = ATTN_RADIUS * dil
        per = c // hb
        last = l // hb - 1
        mode = dict(pipeline_mode=pl.Buffered(1)) if hb >= c else {}
        for which in (1, 2):
            colblk = which * ATTN_GROUPS + g
            in_specs.append(pl.BlockSpec((None, hb, ATTN_OUT),
                                         lambda bi, i, per=per, colblk=colblk: (bi, jnp.maximum(i * per - 1, 0), colblk),
                                         **mode))
            in_specs.append(pl.BlockSpec((None, hb, ATTN_OUT),
                                         lambda bi, i, per=per, last=last, colblk=colblk:
                                         (bi, jnp.minimum((i + 1) * per, last), colblk), **mode))
            args += [qkv, qkv]
    in_specs.append(pl.BlockSpec((ATTN_GROUPS, ATTN_HEADS, ATTN_QB, ATTN_KB), lambda bi, i: (0, 0, 0, 0),
                                 pipeline_mode=pl.Buffered(1)))
    args.append(bias)
    ext = max(c + 2 * ATTN_RADIUS * DILATIONS[-1], ATTN_KB * DILATIONS[-1])
    head_rows = lambda n: pltpu.VMEM((ATTN_HEADS, n, ATTN_HEAD_DIM), F32)
    return pl.pallas_call(
        functools.partial(_attn_kernel, seq_len=l),
        grid=(b, l // c),
        in_specs=in_specs,
        out_specs=pl.BlockSpec((None, c, ATTN_OUT), lambda bi, i: (bi, i, 0)),
        out_shape=jax.ShapeDtypeStruct((b, l, ATTN_OUT), BF16),
        scratch_shapes=[head_rows(c), pltpu.VMEM((ATTN_HEADS, ext, ATTN_HEAD_DIM), jnp.uint32),
                        head_rows(c), head_rows(c), head_rows(c)],
        compiler_params=_params(("parallel", "arbitrary"), 56),
        name="dilated_attention",
    )(*args)


def _merge_kernel(ya_ref, yb_ref, yc_ref, gl_ref, pa_ref, pb_ref, pc_ref, g0_ref, g1_ref, g2_ref,
                  b0_ref, b1_ref, b2_ref, o_ref):
    gl = gl_ref[...]
    dot = lambda x, w_ref: jnp.dot(x, w_ref[...], preferred_element_type=F32)
    acc = _sigmoid(dot(gl, g0_ref) + b0_ref[...]) * dot(ya_ref[...], pa_ref)
    acc += _sigmoid(dot(gl, g1_ref) + b1_ref[...]) * dot(yb_ref[...], pb_ref)
    acc += _sigmoid(dot(gl, g2_ref) + b2_ref[...]) * dot(yc_ref[...], pc_ref)
    o_ref[...] = acc.astype(o_ref.dtype)


def _gated_merge(y_a, y_b, y_c, g_low, proj_a, proj_b, proj_c, gate_up, gate_b, layer, tm, tn, g_low_col=0):
    m = y_a.shape[0]
    d = proj_a.shape[2]
    nj = d // tn
    act = lambda k: pl.BlockSpec((tm, k), lambda i, j: (i, 0))
    wt = lambda k, o: _weight_spec(k, tn, layer, lambda j: j + o * nj)
    gate_b2 = gate_b.reshape(DEPTH, 1, 3 * d)
    return pl.pallas_call(
        _merge_kernel,
        grid=(m // tm, nj),
        in_specs=[act(y_a.shape[1]), act(y_b.shape[1]), act(y_c.shape[1]),
                  pl.BlockSpec((tm, GATE_RANK), lambda i, j: (i, g_low_col)),
                  wt(proj_a.shape[1], 0), wt(proj_b.shape[1], 0), wt(proj_c.shape[1], 0),
                  wt(GATE_RANK, 0), wt(GATE_RANK, 1), wt(GATE_RANK, 2),
                  wt(1, 0), wt(1, 1), wt(1, 2)],
        out_specs=pl.BlockSpec((tm, tn), lambda i, j: (i, j)),
        out_shape=jax.ShapeDtypeStruct((m, d), BF16),
        compiler_params=_params(("parallel", "arbitrary"), 56),
        name="gated_merge",
    )(y_a, y_b, y_c, g_low, proj_a, proj_b, proj_c, gate_up, gate_up, gate_up, gate_b2, gate_b2, gate_b2)


def _reorder_w_in(w_in):
    o = np.cumsum((0, POOL_WIDTH, SSD_INNER, SSD_XBC, 2 * SSD_HEADS, ATTN_QKV, GATE_RANK))
    main = jnp.concatenate([w_in[..., o[4]:o[5]], w_in[..., o[0]:o[3]], w_in[..., o[5]:o[6]], w_in[..., o[3]:o[4]]],
                           axis=-1)
    return jnp.pad(main.astype(BF16), ((0, 0), (0, 0), (0, _WIN_TILES * _WIN_TN - main.shape[-1])))


def kernel(x, c, ada_w, ada_b, ada_layer, t5_table, norm_mix, norm_mlp, w_in, pool_w, pool_scale, ssd_conv_w, ssd_conv_b, ssd_dt_bias, ssd_a_log, ssd_d, ssd_norm, q_norm, k_norm, proj_a, proj_b, proj_c, gate_up, gate_b, w_out, mlp_up, mlp_conv_w, mlp_conv_b, mlp_down):
    b, l, d = x.shape
    m = b * l
    mod = _modulation(c, ada_w, ada_b, ada_layer)
    bias = _t5_bias(t5_table)
    w_in_r = _reorder_w_in(w_in)
    pool_w, proj_a, proj_b, proj_c, gate_up, w_out, mlp_up, mlp_down = (
        w.astype(BF16) for w in (pool_w, proj_a, proj_b, proj_c, gate_up, w_out, mlp_up, mlp_down))
    for layer in range(DEPTH):
        shift_m, scale_m, gate_m, shift_f, scale_f, gate_f = (mod[layer, :, k] for k in range(N_MOD))
        h = _norm_mod(x, norm_mix[layer], scale_m, shift_m)
        proj, dt_raw = _in_projection(h.reshape(m, d), w_in_r, layer, q_norm[layer], k_norm[layer], 1024)
        seq = lambda t: t.reshape(b, l, t.shape[-1])
        y_a = _pool_mixer(seq(proj), pool_w[layer], pool_scale[layer], COL_A_IN * _WIN_TN // POOL_WIDTH)
        y_b = _ssd_mixer(seq(proj), seq(dt_raw), ssd_conv_w[layer], ssd_conv_b[layer], ssd_dt_bias[layer],
                         ssd_a_log[layer], ssd_d[layer], ssd_norm[layer])
        y_c = _attention_mixer(seq(proj), bias)
        merged = _gated_merge(y_a.reshape(m, -1), y_b.reshape(m, -1), y_c.reshape(m, -1), proj,
                              proj_a, proj_b, proj_c, gate_up, gate_b, layer, 1024, 512, COL_G_LOW)
        x = _matmul_residual(merged, w_out, layer, x.reshape(m, d), gate_m, 1024, 512, "out_proj").reshape(b, l, d)
        h = _norm_mod(x, norm_mlp[layer], scale_f, shift_f)
        act = _ffn_up_act(h.reshape(m, d), mlp_up, layer, mlp_conv_w[layer], mlp_conv_b[layer], l, 1024, 512)
        x = _matmul_residual(act, mlp_down, layer, x.reshape(m, d), gate_f, 512, 512, "ffn_down").reshape(b, l, d)
    return x
```

```python
import functools

import numpy as np
import jax
import jax.numpy as jnp
from jax import lax
from jax.experimental import pallas as pl
from jax.experimental.pallas import tpu as pltpu

F32 = jnp.float32
BF16 = jnp.bfloat16

D_MODEL = 4096
DEPTH = 4
N_MOD = 6
EPS = 1e-6
NEG_INF = -1e30
LOG2_E = 1.4426950408889634

POOL_WINDOWS = (2, 4, 8, 16)
POOL_WIDTH = 1536
POOL_GROUP_DIM = 384

SSD_INNER = 2048
SSD_HEADS = 32
SSD_HEAD_DIM = 64
SSD_GROUPS = 4
SSD_STATE = 128
SSD_CHUNK = 128
SSD_BC = 512
SSD_XBC = 3072
SSD_GROUP_WIDTH = SSD_INNER // SSD_GROUPS

DILATIONS = (1, 4, 16)
ATTN_RADIUS = 64
ATTN_GROUPS = 3
ATTN_HEADS = 4
ATTN_HEAD_DIM = 128
ATTN_OUT = 512
ATTN_QKV = 4608
T5_BUCKETS = 32
T5_MAX_DISTANCE = 1024
GATE_RANK = 512
MLP_HIDDEN = 8192

HALO = 16
LANE = 128
MIB = 1024 * 1024


def _params(sem, vmem_mib):
    return pltpu.CompilerParams(dimension_semantics=sem, vmem_limit_bytes=vmem_mib * MIB)


def _sigmoid(x):
    return 1.0 / (1.0 + jnp.exp(-x))


def _silu(x):
    return x * _sigmoid(x)


def _softplus(x):
    return jnp.maximum(x, 0.0) + jnp.log1p(jnp.exp(-jnp.abs(x)))


def _mod_kernel(c_ref, w_ref, b_ref, al_ref, o_ref):
    n_batch = c_ref.shape[0]
    tn = w_ref.shape[1]

    @pl.when(pl.program_id(1) == 0)
    def _():
        o_ref[...] = jnp.broadcast_to((b_ref[...] + al_ref[...])[:, None, :], o_ref.shape)

    rows = []
    for bi in range(n_batch):
        s = _silu(c_ref[bi])
        parts = [jnp.sum(w_ref[:, t * LANE:(t + 1) * LANE] * s, axis=0, keepdims=True) for t in range(tn // LANE)]
        rows.append(jnp.concatenate(parts, axis=1))
    part = jnp.concatenate(rows + [jnp.zeros((8 - n_batch, tn), F32)], axis=0)
    o_ref[...] += part[None, :, :]


def _modulation(c, ada_w, ada_b, ada_layer):
    b = c.shape[0]
    n = ada_w.shape[1]
    tk, tn = 1024, 2048
    c_lanes = jnp.broadcast_to(c[:, :, None], (b, D_MODEL, LANE))
    out = pl.pallas_call(
        _mod_kernel,
        grid=(n // tn, D_MODEL // tk),
        in_specs=[
            pl.BlockSpec((b, tk, LANE), lambda j, k: (0, k, 0)),
            pl.BlockSpec((tk, tn), lambda j, k: (k, j)),
            pl.BlockSpec((1, tn), lambda j, k: (0, j)),
            pl.BlockSpec((DEPTH, tn), lambda j, k: (0, j)),
        ],
        out_specs=pl.BlockSpec((DEPTH, 8, tn), lambda j, k: (0, 0, j)),
        out_shape=jax.ShapeDtypeStruct((DEPTH, 8, n), F32),
        compiler_params=_params(("parallel", "arbitrary"), 40),
        name="adaln_mod",
    )(c_lanes, ada_w, ada_b.reshape(1, n), ada_layer)
    return out[:, :b].reshape(DEPTH, b, N_MOD, 1, D_MODEL)


def _norm_kernel(x_ref, g_ref, sc_ref, sh_ref, o_ref):
    x = x_ref[...]
    ms = jnp.mean(x * x, axis=-1, keepdims=True)
    y = x * lax.rsqrt(ms + EPS) * g_ref[...]
    o_ref[...] = (y * (1.0 + sc_ref[...]) + sh_ref[...]).astype(o_ref.dtype)


def _norm_mod(x, g, scale, shift):
    b, l, d = x.shape
    ts = 512
    return pl.pallas_call(
        _norm_kernel,
        grid=(b, l // ts),
        in_specs=[
            pl.BlockSpec((None, ts, d), lambda bi, i: (bi, i, 0)),
            pl.BlockSpec((1, d), lambda bi, i: (0, 0)),
            pl.BlockSpec((None, 1, d), lambda bi, i: (bi, 0, 0)),
            pl.BlockSpec((None, 1, d), lambda bi, i: (bi, 0, 0)),
        ],
        out_specs=pl.BlockSpec((None, ts, d), lambda bi, i: (bi, i, 0)),
        out_shape=jax.ShapeDtypeStruct((b, l, d), BF16),
        compiler_params=_params(("parallel", "parallel"), 40),
        name="norm_mod",
    )(x, g.reshape(1, d), scale, shift)


def _weight_spec(k, tn, layer, col=lambda j: j):
    return pl.BlockSpec((None, k, tn), lambda i, j: (layer, 0, col(j)))


def _mm_res_kernel(a_ref, b_ref, x_ref, g_ref, o_ref):
    acc = jnp.dot(a_ref[...], b_ref[...], preferred_element_type=F32)
    o_ref[...] = x_ref[...] + g_ref[...] * acc


def _matmul_residual(a, w, layer, x, gate, tm, tn, name):
    m, k = a.shape
    n = w.shape[2]
    blocks_per_batch = (m // gate.shape[0]) // tm
    return pl.pallas_call(
        _mm_res_kernel,
        grid=(m // tm, n // tn),
        in_specs=[
            pl.BlockSpec((tm, k), lambda i, j: (i, 0)),
            _weight_spec(k, tn, layer),
            pl.BlockSpec((tm, tn), lambda i, j: (i, j)),
            pl.BlockSpec((None, 1, tn), lambda i, j: (i // blocks_per_batch, 0, j)),
        ],
        out_specs=pl.BlockSpec((tm, tn), lambda i, j: (i, j)),
        out_shape=jax.ShapeDtypeStruct((m, n), F32),
        compiler_params=_params(("parallel", "arbitrary"), 56),
        name=name,
    )(a, w, x, gate)


_WIN_TN = 512
_WIN_SEGMENTS = (ATTN_QKV, POOL_WIDTH, SSD_INNER, SSD_XBC, GATE_RANK)
_WIN_TILE_START = tuple(int(s) // _WIN_TN for s in np.cumsum((0,) + _WIN_SEGMENTS))
_WIN_TILES = _WIN_TILE_START[-1] + 1
_WIN_QKV = 0
DT_LANES = LANE


def _head_rmsnorm(x, gain):
    out = []
    for h in range(ATTN_HEADS):
        xh = x[:, h * ATTN_HEAD_DIM:(h + 1) * ATTN_HEAD_DIM]
        ms = jnp.mean(xh * xh, axis=-1, keepdims=True)
        out.append(xh * lax.rsqrt(ms + EPS) * gain)
    return jnp.concatenate(out, axis=1)


_WIN_Q0 = _WIN_TILE_START[_WIN_QKV]
_WIN_V0 = _WIN_Q0 + 2 * ATTN_GROUPS
_, COL_A_IN, COL_Z, COL_XBC, COL_G_LOW = _WIN_TILE_START[:5]


def _win_kernel(a_ref, b_ref, gain_ref, o_ref, dt_ref):
    j = pl.program_id(1)
    normed = (j >= _WIN_Q0) & (j < _WIN_V0)
    last = j == _WIN_TILES - 1

    @pl.when(last)
    def _():
        dt_ref[...] = jnp.dot(a_ref[...], b_ref[:, :DT_LANES], preferred_element_type=F32)

    @pl.when(normed)
    def _():
        acc = jnp.dot(a_ref[...], b_ref[...], preferred_element_type=F32)
        o_ref[...] = _head_rmsnorm(acc, gain_ref[...]).astype(o_ref.dtype)

    @pl.when(jnp.logical_not(normed | last))
    def _():
        o_ref[...] = jnp.dot(a_ref[...], b_ref[...], preferred_element_type=F32).astype(o_ref.dtype)


def _in_projection(h, w_in_r, layer, q_norm, k_norm, tm):
    m, k = h.shape
    n_main = (_WIN_TILES - 1) * _WIN_TN
    gains = jnp.stack([q_norm * ATTN_HEAD_DIM ** -0.5, k_norm]).reshape(2, 1, ATTN_HEAD_DIM)
    return pl.pallas_call(
        _win_kernel,
        grid=(m // tm, _WIN_TILES),
        in_specs=[
            pl.BlockSpec((tm, k), lambda i, j: (i, 0)),
            _weight_spec(k, _WIN_TN, layer),
            pl.BlockSpec((None, 1, ATTN_HEAD_DIM), lambda i, j: (jnp.clip((j - _WIN_Q0) // ATTN_GROUPS, 0, 1), 0, 0)),
        ],
        out_specs=[pl.BlockSpec((tm, _WIN_TN), lambda i, j: (i, jnp.minimum(j, _WIN_TILES - 2))),
                   pl.BlockSpec((tm, DT_LANES), lambda i, j: (i, 0))],
        out_shape=[jax.ShapeDtypeStruct((m, n_main), BF16), jax.ShapeDtypeStruct((m, DT_LANES), F32)],
        compiler_params=_params(("parallel", "arbitrary"), 56),
        name="in_proj",
    )(h, w_in_r, gains)


FFN_CONV = 3


def _ffn_up_kernel(hm_ref, hp_ref, hn_ref, wu_ref, wv_ref, cwu_ref, cbu_ref, cwv_ref, cbv_ref, o_ref,
                   a_s, u_s, v_s, *, tm, blocks_per_seq):
    @pl.when(pl.program_id(1) == 0)
    def _():
        pos = pl.program_id(0) % blocks_per_seq
        a_s[0:HALO] = jnp.where(pos > 0, hp_ref[...], jnp.zeros_like(hp_ref))
        a_s[HALO:HALO + tm] = hm_ref[...]
        a_s[HALO + tm:] = jnp.where(pos < blocks_per_seq - 1, hn_ref[...], jnp.zeros_like(hn_ref))

    a = a_s[...]
    u_s[...] = jnp.dot(a, wu_ref[...], preferred_element_type=F32)
    v_s[...] = jnp.dot(a, wv_ref[...], preferred_element_type=F32)

    def conv(s_ref, w_ref, b_ref):
        acc = b_ref[...]
        for k in range(FFN_CONV):
            r0 = HALO + k - FFN_CONV // 2
            acc = acc + s_ref[r0:r0 + tm, :] * w_ref[k:k + 1, :]
        return acc

    o_ref[...] = (_silu(conv(u_s, cwu_ref, cbu_ref)) * conv(v_s, cwv_ref, cbv_ref)).astype(o_ref.dtype)


def _ffn_up_act(h, w_up, layer, conv_w, conv_b, seq_len, tm, tn):
    m, k = h.shape
    hid = w_up.shape[2] // 2
    off = hid // tn
    r = tm // HALO
    last = m // HALO - 1
    conv_b2 = conv_b.reshape(1, 2 * hid)
    cw = lambda o: pl.BlockSpec((FFN_CONV, tn), lambda i, j: (0, j + o))
    cb = lambda o: pl.BlockSpec((1, tn), lambda i, j: (0, j + o))
    ext = tm + 2 * HALO
    return pl.pallas_call(
        functools.partial(_ffn_up_kernel, tm=tm, blocks_per_seq=seq_len // tm),
        grid=(m // tm, hid // tn),
        in_specs=[
            pl.BlockSpec((tm, k), lambda i, j: (i, 0), pipeline_mode=pl.Buffered(1)),
            pl.BlockSpec((HALO, k), lambda i, j: (jnp.maximum(i * r - 1, 0), 0)),
            pl.BlockSpec((HALO, k), lambda i, j: (jnp.minimum((i + 1) * r, last), 0)),
            _weight_spec(k, tn, layer), _weight_spec(k, tn, layer, lambda j: j + off),
            cw(0), cb(0), cw(off), cb(off),
        ],
        out_specs=pl.BlockSpec((tm, tn), lambda i, j: (i, j)),
        out_shape=jax.ShapeDtypeStruct((m, hid), BF16),
        scratch_shapes=[pltpu.VMEM((ext, k), BF16), pltpu.VMEM((ext, tn), F32), pltpu.VMEM((ext, tn), F32)],
        compiler_params=_params(("parallel", "arbitrary"), 56),
        name="ffn_up_conv_act",
    )(h, h, h, w_up, w_up, conv_w, conv_b2, conv_w, conv_b2)


def _halo_specs(ts, width, seq_len, col):
    r = ts // HALO
    last = seq_len // HALO - 1
    return [
        pl.BlockSpec((None, ts, width), lambda b, i, j: (b, i, col(j))),
        pl.BlockSpec((None, HALO, width), lambda b, i, j: (b, jnp.maximum(i * r - 1, 0), col(j))),
        pl.BlockSpec((None, HALO, width), lambda b, i, j: (b, jnp.minimum((i + 1) * r, last), col(j))),
    ]


ROW_BLOCK = 128
ROW_WINDOW = ROW_BLOCK + 2 * HALO


def _fill_extended(ext_s, main_ref, prev_ref, next_ref, i, n_tiles):
    ts = main_ref.shape[0]
    ext_s[0:HALO] = jnp.where(i > 0, prev_ref[...], jnp.zeros_like(prev_ref))
    ext_s[HALO:HALO + ts] = main_ref[...]
    ext_s[HALO + ts:] = jnp.where(i < n_tiles - 1, next_ref[...], jnp.zeros_like(next_ref))


def _row_mixers(offset_ranges):
    t = lax.broadcasted_iota(jnp.int32, (ROW_BLOCK, ROW_WINDOW), 0)
    j = lax.broadcasted_iota(jnp.int32, (ROW_BLOCK, ROW_WINDOW), 1) - HALO
    mats = [jnp.where((j >= t + lo) & (j <= t + hi), 1.0, 0.0).astype(BF16) for lo, hi in offset_ranges]
    return jnp.concatenate(mats, axis=0)


def _shift_rows(ext, k):
    n = ext.shape[0]
    return pltpu.roll(ext, (-k) % n, axis=0)


def _pool_kernel(m_ref, p_ref, n_ref, w_ref, sc_ref, o_ref, *, ts, seq_len):
    i = pl.program_id(1)
    prev = jnp.where(i > 0, p_ref[...].astype(F32), 0.0)
    nxt = jnp.where(i < seq_len // ts - 1, n_ref[...].astype(F32), 0.0)
    ext = jnp.concatenate([prev, m_ref[...].astype(F32), nxt], axis=0)
    centre = lambda t: t[HALO:HALO + ts]
    pos = i * ts + lax.broadcasted_iota(jnp.int32, (ts, 1), 0)
    for gi, win in enumerate(POOL_WINDOWS):
        sl = slice(gi * POOL_GROUP_DIM, (gi + 1) * POOL_GROUP_DIM)
        e = ext[:, sl]
        acc = _shift_rows(e, -1) + e
        w = 2
        while w < win:
            acc = _shift_rows(acc, -(w // 2)) + _shift_rows(acc, w // 2)
            w *= 2
        lo = jnp.maximum(pos - win // 2, 0)
        hi = jnp.minimum(pos + win - win // 2, seq_len)
        cnt = (hi - lo).astype(F32)
        p = (centre(acc) / cnt - centre(e)).astype(BF16)
        y = jnp.dot(p, w_ref[gi], preferred_element_type=F32)
        o_ref[:, sl] = (y * sc_ref[:, sl]).astype(o_ref.dtype)


def _pool_mixer(a_in, pool_w, pool_scale, col_block=0):
    b, l, _ = a_in.shape
    c = POOL_WIDTH
    ts = 512
    kern = functools.partial(_pool_kernel, ts=ts, seq_len=l)
    return pl.pallas_call(
        kern,
        grid=(b, l // ts, 1),
        in_specs=_halo_specs(ts, c, l, lambda j: col_block) + [
            pl.BlockSpec((len(POOL_WINDOWS), POOL_GROUP_DIM, POOL_GROUP_DIM), lambda bi, i, j: (0, 0, 0)),
            pl.BlockSpec((1, c), lambda bi, i, j: (0, 0)),
        ],
        out_specs=pl.BlockSpec((None, ts, c), lambda bi, i, j: (bi, i, 0)),
        out_shape=jax.ShapeDtypeStruct((b, l, c), BF16),
        compiler_params=_params(("parallel", "parallel", "arbitrary"), 48),
        name="pool_mixer",
    )(a_in, a_in, a_in, pool_w, pool_scale.reshape(1, c))


def _ssd_conv_kernel(m_ref, p_ref, n_ref, w_ref, b_ref, o_ref, ext_s, *, ts, seq_len):
    _fill_extended(ext_s, m_ref, p_ref, n_ref, pl.program_id(1), seq_len // ts)
    width = w_ref.shape[0]
    taps = [k - width // 2 for k in range(width)]
    shifts = _row_mixers([(d, d) for d in taps if d != 0])
    for blk in range(ts // ROW_BLOCK):
        r0 = blk * ROW_BLOCK
        moved = jnp.dot(shifts, ext_s[r0:r0 + ROW_WINDOW, :], preferred_element_type=F32)
        acc = b_ref[...]
        m = 0
        for k, d in enumerate(taps):
            if d == 0:
                xk = ext_s[HALO + r0:HALO + r0 + ROW_BLOCK, :].astype(F32)
            else:
                xk = moved[m * ROW_BLOCK:(m + 1) * ROW_BLOCK]
                m += 1
            acc = acc + xk * w_ref[k:k + 1, :]
        o_ref[r0:r0 + ROW_BLOCK, :] = _silu(acc).astype(o_ref.dtype)


def _ssd_conv(xbc, conv_w, conv_b, col0=0):
    b, l, _ = xbc.shape
    c = SSD_XBC
    ts, tc = 1024, 512
    kern = functools.partial(_ssd_conv_kernel, ts=ts, seq_len=l)
    return pl.pallas_call(
        kern,
        grid=(b, l // ts, c // tc),
        in_specs=_halo_specs(ts, tc, l, lambda j: j + col0) + [
            pl.BlockSpec((conv_w.shape[0], tc), lambda bi, i, j: (0, j)),
            pl.BlockSpec((1, tc), lambda bi, i, j: (0, j)),
        ],
        out_specs=pl.BlockSpec((None, ts, tc), lambda bi, i, j: (bi, i, j)),
        out_shape=jax.ShapeDtypeStruct((b, l, c), BF16),
        scratch_shapes=[pltpu.VMEM((ts + 2 * HALO, tc), BF16)],
        compiler_params=_params(("parallel", "parallel", "arbitrary"), 32),
        name="ssd_conv",
    )(xbc, xbc, xbc, conv_w, conv_b.reshape(1, c))


def _split_dot(lhs, rhs, terms, split_lhs):
    x = lhs if split_lhs else rhs
    acc = None
    for _ in range(terms):
        hi = x.astype(BF16)
        part = (jnp.dot(hi, rhs, preferred_element_type=F32) if split_lhs
                else jnp.dot(lhs, hi, preferred_element_type=F32))
        acc = part if acc is None else acc + part
        x = x - hi.astype(F32)
    return acc


def _ssd_kernel(xs_ref, bm_ref, cm_ref, dtr_ref, dtb_ref, alog_ref, e_ref, *rest, reverse, final):
    if final:
        yf_ref, *z_refs, dsk_ref, ng_ref, o_ref, st_ref = rest
    else:
        o_ref, st_ref = rest
    tc = SSD_CHUNK
    gw = SSD_GROUP_WIDTH

    @pl.when(pl.program_id(1) == 0)
    def _():
        st_ref[...] = jnp.zeros_like(st_ref)

    lane0 = SSD_HEADS if reverse else 0
    dt = _softplus(dtr_ref[...] + dtb_ref[...])
    dta = dt * (-jnp.exp(alog_ref[...]))
    ti = lax.broadcasted_iota(jnp.int32, (tc, tc), 0)
    ui = lax.broadcasted_iota(jnp.int32, (tc, tc), 1)
    causal = (ui >= ti) if reverse else (ui <= ti)
    cs = _split_dot(jnp.where(causal, 1.0, 0.0).astype(BF16), dta, 3, split_lhs=False)
    total = cs[0:1] if reverse else cs[tc - 1:tc]
    w1 = dt * jnp.exp(total - cs)
    chunk_decay = jnp.broadcast_to(jnp.exp(total), (8, LANE))
    cd_e = _split_dot(chunk_decay, e_ref[...], 2, split_lhs=True)[0:1]
    cs2 = cs * LOG2_E
    src2_t = (cs2 - jnp.log2(dt)).T
    w1_t = w1.T
    lane = lax.broadcasted_iota(jnp.int32, (tc, LANE), 1)
    first_head = lane < SSD_HEAD_DIM

    def two_heads(x):
        zero = jnp.zeros_like(x)
        return jnp.where(first_head, x, zero), jnp.where(first_head, zero, x)

    for g in range(SSD_GROUPS):
        gs = slice(g * gw, (g + 1) * gw)
        bm = bm_ref[:, g * SSD_STATE:(g + 1) * SSD_STATE]
        cm = cm_ref[:, g * SSD_STATE:(g + 1) * SSD_STATE]
        cb = lax.dot_general(cm, bm, (((1,), (1,)), ((), ())), preferred_element_type=F32)
        bm_t = bm.astype(F32).T
        cm32 = cm.astype(F32)
        s_prev = st_ref[g]
        s_prev16 = s_prev.astype(BF16)
        parts, states = [], []
        for pr in range(gw // LANE):
            h0 = g * (gw // SSD_HEAD_DIM) + 2 * pr
            lhs_y, lhs_s = [], []
            for h in (h0, h0 + 1):
                hl = slice(lane0 + h, lane0 + h + 1)
                col = jnp.broadcast_to(cs2[:, hl], (tc, tc))
                decay_dt = jnp.exp2(jnp.where(causal, col - src2_t[hl, :], NEG_INF))
                lhs_y.append((cb * decay_dt).astype(BF16))
                lhs_y.append((cm32 * jnp.exp2(col)).astype(BF16))
                lhs_s.append((bm_t * w1_t[lane0 + h:lane0 + h + 1, :]).astype(BF16))
            ps = slice(pr * LANE, (pr + 1) * LANE)
            x_top, x_bot = two_heads(xs_ref[:, h0 * SSD_HEAD_DIM:(h0 + 2) * SSD_HEAD_DIM])
            s_top, s_bot = two_heads(s_prev16[:, ps])
            rhs_y = jnp.concatenate([x_top, s_top, x_bot, s_bot], axis=0)
            parts.append(jnp.dot(jnp.concatenate(lhs_y, axis=1), rhs_y, preferred_element_type=F32))
            states.append(jnp.dot(jnp.concatenate(lhs_s, axis=1), jnp.concatenate([x_top, x_bot], axis=0),
                                  preferred_element_type=F32))
        st_ref[g] = s_prev * cd_e[:, gs] + jnp.concatenate(states, axis=1)
        y = jnp.concatenate(parts, axis=1)
        if final:
            y = y + yf_ref[:, gs] + xs_ref[:, gs].astype(F32) * dsk_ref[:, gs]
            y = y * _silu(z_refs[g][...].astype(F32))
            y = y * lax.rsqrt(jnp.mean(y * y, axis=-1, keepdims=True) + EPS)
            o_ref[:, gs] = (y * ng_ref[:, gs]).astype(o_ref.dtype)
        else:
            o_ref[:, gs] = y


def _head_expander(reverse):
    e = np.zeros((LANE, SSD_INNER), np.float32)
    lane0 = SSD_HEADS if reverse else 0
    for h in range(SSD_HEADS):
        e[lane0 + h, h * SSD_HEAD_DIM:(h + 1) * SSD_HEAD_DIM] = 1.0
    return jnp.asarray(e, BF16)


def _ssd_pass(xbc_act, dt_raw, dt_bias_row, a_log_row, reverse, final_args=None):
    b, l, _ = xbc_act.shape
    tc = SSD_CHUNK
    nc = l // tc
    final = final_args is not None
    chunk = (lambda c: nc - 1 - c) if reverse else (lambda c: c)
    seq = lambda width, colblk: pl.BlockSpec((None, tc, width), lambda bi, c: (bi, chunk(c), colblk))
    row = lambda width: pl.BlockSpec((1, width), lambda bi, c: (0, 0))
    in_specs = [
        seq(SSD_INNER, 0),
        seq(SSD_BC, SSD_INNER // SSD_BC),
        seq(SSD_BC, SSD_INNER // SSD_BC + 1),
        seq(DT_LANES, 0),
        row(DT_LANES), row(DT_LANES),
        pl.BlockSpec((LANE, SSD_INNER), lambda bi, c: (0, 0)),
    ]
    args = [xbc_act, xbc_act, xbc_act, dt_raw, dt_bias_row, a_log_row, _head_expander(reverse)]
    if final:
        y_fwd, z, z_col0, d_skip_row, norm_row = final_args
        in_specs += [seq(SSD_INNER, 0)] + [seq(SSD_GROUP_WIDTH, z_col0 + g) for g in range(SSD_GROUPS)]
        in_specs += [row(SSD_INNER), row(SSD_INNER)]
        args += [y_fwd] + [z] * SSD_GROUPS + [d_skip_row, norm_row]
    return pl.pallas_call(
        functools.partial(_ssd_kernel, reverse=reverse, final=final),
        grid=(b, nc),
        in_specs=in_specs,
        out_specs=seq(SSD_INNER, 0),
        out_shape=jax.ShapeDtypeStruct((b, l, SSD_INNER), BF16 if final else F32),
        scratch_shapes=[pltpu.VMEM((SSD_GROUPS, SSD_STATE, SSD_GROUP_WIDTH), F32)],
        compiler_params=_params(("parallel", "arbitrary"), 32),
        name="ssd_bwd_final" if final else "ssd_fwd",
    )(*args)


def _ssd_mixer(proj, dt_raw, conv_w, conv_b, dt_bias, a_log, d_skip, norm_g, z_col0=COL_Z, xbc_col0=COL_XBC):
    xbc_act = _ssd_conv(proj, conv_w, conv_b, xbc_col0)
    pad = DT_LANES - 2 * SSD_HEADS
    dtb = jnp.pad(dt_bias.reshape(1, 2 * SSD_HEADS), ((0, 0), (0, pad)))
    alog = jnp.pad(a_log.reshape(1, 2 * SSD_HEADS), ((0, 0), (0, pad)))
    d_row = jnp.repeat(d_skip, SSD_HEAD_DIM).reshape(1, SSD_INNER)
    y_fwd = _ssd_pass(xbc_act, dt_raw, dtb, alog, reverse=False)
    return _ssd_pass(xbc_act, dt_raw, dtb, alog, reverse=True,
                     final_args=(y_fwd, proj, z_col0, d_row, norm_g.reshape(1, SSD_INNER)))


ATTN_QB = 128
ATTN_KB = ATTN_QB + 2 * ATTN_RADIUS


def _t5_bucket_table():
    half = T5_BUCKETS // 2
    max_exact = half // 2
    delta = np.arange(ATTN_KB)[None, :] - ATTN_RADIUS - np.arange(ATTN_QB)[:, None]
    out = []
    for dil in DILATIONS:
        rel = delta * dil
        n = np.abs(rel)
        large = max_exact + (np.log(np.maximum(n, max_exact) / max_exact) / np.log(T5_MAX_DISTANCE / max_exact)
                             * (half - max_exact)).astype(np.int32)
        large = np.minimum(large, half - 1)
        out.append((rel > 0).astype(np.int32) * half + np.where(n < max_exact, n, large).astype(np.int32))
    return np.stack(out).astype(np.int32)


def _bias_kernel(tab_ref, bkt_ref, o_ref):
    g = pl.program_id(0)
    bk = bkt_ref[...]
    for h in range(ATTN_HEADS):
        acc = jnp.zeros(bk.shape, F32)
        for bucket in range(T5_BUCKETS):
            acc = jnp.where(bk == bucket, tab_ref[bucket, g * ATTN_HEADS + h], acc)
        o_ref[h] = acc


def _t5_bias(t5_table):
    return pl.pallas_call(
        _bias_kernel,
        grid=(ATTN_GROUPS,),
        in_specs=[
            pl.BlockSpec(memory_space=pltpu.SMEM),
            pl.BlockSpec((None, ATTN_QB, ATTN_KB), lambda g: (g, 0, 0)),
        ],
        out_specs=pl.BlockSpec((None, ATTN_HEADS, ATTN_QB, ATTN_KB), lambda g: (g, 0, 0, 0)),
        out_shape=jax.ShapeDtypeStruct((ATTN_GROUPS, ATTN_HEADS, ATTN_QB, ATTN_KB), F32),
        compiler_params=_params(("arbitrary",), 16),
        name="t5_bias",
    )(t5_table, jnp.asarray(_t5_bucket_table()))


ATTN_CHUNK = 1024


_HIGH_HALF = 0xFFFF0000


def _attn_fill(dst, row0, src_ref, col0, n_rows, low_ref=None, low_col0=0):
    step = min(n_rows, LANE)

    def body(c, carry):
        r = pl.multiple_of(c * step, step)
        for h in range(ATTN_HEADS):
            x = src_ref[pl.ds(r, step), col0 + h * ATTN_HEAD_DIM:col0 + (h + 1) * ATTN_HEAD_DIM].astype(F32)
            if low_ref is not None:
                y = low_ref[pl.ds(r, step), low_col0 + h * ATTN_HEAD_DIM:low_col0 + (h + 1) * ATTN_HEAD_DIM]
                x = lax.bitcast_convert_type(x, jnp.uint32) | (lax.bitcast_convert_type(y.astype(F32), jnp.uint32) >> 16)
            dst[h, pl.ds(row0 + r, step), :] = x
        return carry

    lax.fori_loop(0, n_rows // step, body, 0)


def _attn_kernel(q_ref, k_ref, v_ref, *rest, seq_len):
    halos = rest[:4 * ATTN_GROUPS]
    bias_ref, o_ref, qn_s, kv_s, m_s, d_s, n_s = rest[4 * ATTN_GROUPS:]
    i = pl.program_id(1)
    n_chunks = seq_len // ATTN_CHUNK
    r_ = ATTN_RADIUS
    ones = jnp.ones((ATTN_KB, ATTN_HEAD_DIM), BF16)
    tail = kv_s.shape[1] - (ATTN_CHUNK + 2 * r_ * DILATIONS[-1])
    for h in range(ATTN_HEADS):
        kv_s[h, kv_s.shape[1] - tail:, :] = jnp.zeros((tail, ATTN_HEAD_DIM), jnp.uint32)
    for g, dil in enumerate(DILATIONS):
        hb = r_ * dil
        n_strided = ATTN_CHUNK // dil
        qb = min(ATTN_QB, n_strided)
        kb = ATTN_KB
        nblk = n_strided // qb
        kp_ref, kn_ref, vp_ref, vn_ref = halos[4 * g:4 * g + 4]
        col = g * ATTN_OUT
        _attn_fill(qn_s, 0, q_ref, col, ATTN_CHUNK)
        _attn_fill(kv_s, 0, kp_ref, 0, hb, vp_ref, 0)
        _attn_fill(kv_s, hb, k_ref, col, ATTN_CHUNK, v_ref, col)
        _attn_fill(kv_s, hb + ATTN_CHUNK, kn_ref, 0, hb, vn_ref, 0)

        qq = lax.broadcasted_iota(jnp.int32, (qb, kb), 0)
        kk = lax.broadcasted_iota(jnp.int32, (qb, kb), 1)
        band = jnp.abs(kk - r_ - qq) <= r_

        def block(it, carry, g=g, dil=dil, qb=qb, kb=kb, nblk=nblk, band=band, kk=kk):
            res = it // nblk
            blk = it % nblk
            start = res + blk * (qb * dil)
            lo = jnp.where((i == 0) & (blk == 0), r_, 0)
            hi = jnp.where((i == n_chunks - 1) & (blk == nblk - 1), qb + r_, kb)
            valid = band & (kk >= lo) & (kk < hi)
            rows = pl.ds(start, qb, stride=dil)
            heads = range(ATTN_HEADS)
            qs = [qn_s[h, rows, :] for h in heads]
            kvs = [kv_s[h, pl.ds(start, kb, stride=dil), :] for h in heads]
            old = [(m_s[h, rows, :], d_s[h, rows, :], n_s[h, rows, :]) for h in heads] if g > 0 else None
            new = []
            for h in heads:
                k = lax.bitcast_convert_type(kvs[h] & jnp.uint32(_HIGH_HALF), F32).astype(BF16)
                v = lax.bitcast_convert_type(kvs[h] << 16, F32).astype(BF16)
                s = lax.dot_general(qs[h].astype(BF16), k, (((1,), (1,)), ((), ())), preferred_element_type=F32)
                s = jnp.where(valid, s + bias_ref[g, h, :qb, :kb], NEG_INF)
                mx = jnp.max(s, axis=-1, keepdims=True)
                ex = jnp.exp(s - mx).astype(BF16)
                nd = jnp.dot(ex, jnp.concatenate([v, ones[:kb]], axis=1), preferred_element_type=F32)
                num, den = nd[:, :ATTN_HEAD_DIM], nd[:, ATTN_HEAD_DIM:]
                mx = jnp.broadcast_to(mx, (qb, ATTN_HEAD_DIM))
                if g > 0:
                    m_old, d_old, n_old = old[h]
                    m_new = jnp.maximum(m_old, mx)
                    w_old = jnp.exp(m_old - m_new)
                    w_new = jnp.exp(mx - m_new)
                    num = n_old * w_old + num * w_new
                    den = d_old * w_old + den * w_new
                    mx = m_new
                new.append((mx, den, num))
            for h in heads:
                m_s[h, rows, :], d_s[h, rows, :], n_s[h, rows, :] = new[h]
            return carry

        lax.fori_loop(0, dil * nblk, block, 0, unroll=2 if g == 1 else 4)

    def finish(c, carry):
        r = pl.multiple_of(c * LANE, LANE)
        for h in range(ATTN_HEADS):
            out = n_s[h, pl.ds(r, LANE), :] / d_s[h, pl.ds(r, LANE), :]
            o_ref[pl.ds(r, LANE), h * ATTN_HEAD_DIM:(h + 1) * ATTN_HEAD_DIM] = out.astype(o_ref.dtype)
        return carry

    lax.fori_loop(0, ATTN_CHUNK // LANE, finish, 0)


def _attention_mixer(qkv, bias):
    b, l, _ = qkv.shape
    c = ATTN_CHUNK
    third = ATTN_QKV // 3
    main = lambda which: pl.BlockSpec((None, c, third), lambda bi, i: (bi, i, which))
    in_specs = [main(0), main(1), main(2)]
    args = [qkv, qkv, qkv]
    for g, dil in enumerate(DILATIONS):
        hb = ATTN_RADIUS * dil
        per = c // hb
        last = l // hb - 1
        mode = dict(pipeline_mode=pl.Buffered(1)) if hb >= c else {}
        for which in (1, 2):
            colblk = which * ATTN_GROUPS + g
            in_specs.append(pl.BlockSpec((None, hb, ATTN_OUT),
                                         lambda bi, i, per=per, colblk=colblk: (bi, jnp.maximum(i * per - 1, 0), colblk),
                                         **mode))
            in_specs.append(pl.BlockSpec((None, hb, ATTN_OUT),
                                         lambda bi, i, per=per, last=last, colblk=colblk:
                                         (bi, jnp.minimum((i + 1) * per, last), colblk), **mode))
            args += [qkv, qkv]
    in_specs.append(pl.BlockSpec((ATTN_GROUPS, ATTN_HEADS, ATTN_QB, ATTN_KB), lambda bi, i: (0, 0, 0, 0),
                                 pipeline_mode=pl.Buffered(1)))
    args.append(bias)
    ext = max(c + 2 * ATTN_RADIUS * DILATIONS[-1], ATTN_KB * DILATIONS[-1])
    head_rows = lambda n: pltpu.VMEM((ATTN_HEADS, n, ATTN_HEAD_DIM), F32)
    return pl.pallas_call(
        functools.partial(_attn_kernel, seq_len=l),
        grid=(b, l // c),
        in_specs=in_specs,
        out_specs=pl.BlockSpec((None, c, ATTN_OUT), lambda bi, i: (bi, i, 0)),
        out_shape=jax.ShapeDtypeStruct((b, l, ATTN_OUT), BF16),
        scratch_shapes=[head_rows(c), pltpu.VMEM((ATTN_HEADS, ext, ATTN_HEAD_DIM), jnp.uint32),
                        head_rows(c), head_rows(c), head_rows(c)],
        compiler_params=_params(("parallel", "arbitrary"), 56),
        name="dilated_attention",
    )(*args)


def _merge_kernel(ya_ref, yb_ref, yc_ref, gl_ref, pa_ref, pb_ref, pc_ref, g0_ref, g1_ref, g2_ref,
                  b0_ref, b1_ref, b2_ref, o_ref):
    gl = gl_ref[...]
    dot = lambda x, w_ref: jnp.dot(x, w_ref[...], preferred_element_type=F32)
    acc = _sigmoid(dot(gl, g0_ref) + b0_ref[...]) * dot(ya_ref[...], pa_ref)
    acc += _sigmoid(dot(gl, g1_ref) + b1_ref[...]) * dot(yb_ref[...], pb_ref)
    acc += _sigmoid(dot(gl, g2_ref) + b2_ref[...]) * dot(yc_ref[...], pc_ref)
    o_ref[...] = acc.astype(o_ref.dtype)


def _gated_merge(y_a, y_b, y_c, g_low, proj_a, proj_b, proj_c, gate_up, gate_b, layer, tm, tn, g_low_col=0):
    m = y_a.shape[0]
    d = proj_a.shape[2]
    nj = d // tn
    act = lambda k: pl.BlockSpec((tm, k), lambda i, j: (i, 0))
    wt = lambda k, o: _weight_spec(k, tn, layer, lambda j: j + o * nj)
    gate_b2 = gate_b.reshape(DEPTH, 1, 3 * d)
    return pl.pallas_call(
        _merge_kernel,
        grid=(m // tm, nj),
        in_specs=[act(y_a.shape[1]), act(y_b.shape[1]), act(y_c.shape[1]),
                  pl.BlockSpec((tm, GATE_RANK), lambda i, j: (i, g_low_col)),
                  wt(proj_a.shape[1], 0), wt(proj_b.shape[1], 0), wt(proj_c.shape[1], 0),
                  wt(GATE_RANK, 0), wt(GATE_RANK, 1), wt(GATE_RANK, 2),
                  wt(1, 0), wt(1, 1), wt(1, 2)],
        out_specs=pl.BlockSpec((tm, tn), lambda i, j: (i, j)),
        out_shape=jax.ShapeDtypeStruct((m, d), BF16),
        compiler_params=_params(("parallel", "arbitrary"), 56),
        name="gated_merge",
    )(y_a, y_b, y_c, g_low, proj_a, proj_b, proj_c, gate_up, gate_up, gate_up, gate_b2, gate_b2, gate_b2)


def _reorder_w_in(w_in):
    o = np.cumsum((0, POOL_WIDTH, SSD_INNER, SSD_XBC, 2 * SSD_HEADS, ATTN_QKV, GATE_RANK))
    main = jnp.concatenate([w_in[..., o[4]:o[5]], w_in[..., o[0]:o[3]], w_in[..., o[5]:o[6]], w_in[..., o[3]:o[4]]],
                           axis=-1)
    return jnp.pad(main.astype(BF16), ((0, 0), (0, 0), (0, _WIN_TILES * _WIN_TN - main.shape[-1])))


def kernel(x, c, ada_w, ada_b, ada_layer, t5_table, norm_mix, norm_mlp, w_in, pool_w, pool_scale, ssd_conv_w, ssd_conv_b, ssd_dt_bias, ssd_a_log, ssd_d, ssd_norm, q_norm, k_norm, proj_a, proj_b, proj_c, gate_up, gate_b, w_out, mlp_up, mlp_conv_w, mlp_conv_b, mlp_down):
    b, l, d = x.shape
    m = b * l
    mod = _modulation(c, ada_w, ada_b, ada_layer)
    bias = _t5_bias(t5_table)
    w_in_r = _reorder_w_in(w_in)
    pool_w, proj_a, proj_b, proj_c, gate_up, w_out, mlp_up, mlp_down = (
        w.astype(BF16) for w in (pool_w, proj_a, proj_b, proj_c, gate_up, w_out, mlp_up, mlp_down))
    for layer in range(DEPTH):
        shift_m, scale_m, gate_m, shift_f, scale_f, gate_f = (mod[layer, :, k] for k in range(N_MOD))
        h = _norm_mod(x, norm_mix[layer], scale_m, shift_m)
        proj, dt_raw = _in_projection(h.reshape(m, d), w_in_r, layer, q_norm[layer], k_norm[layer], 1024)
        seq = lambda t: t.reshape(b, l, t.shape[-1])
        y_a = _pool_mixer(seq(proj), pool_w[layer], pool_scale[layer], COL_A_IN * _WIN_TN // POOL_WIDTH)
        y_b = _ssd_mixer(seq(proj), seq(dt_raw), ssd_conv_w[layer], ssd_conv_b[layer], ssd_dt_bias[layer],
                         ssd_a_log[layer], ssd_d[layer], ssd_norm[layer])
        y_c = _attention_mixer(seq(proj), bias)
        merged = _gated_merge(y_a.reshape(m, -1), y_b.reshape(m, -1), y_c.reshape(m, -1), proj,
                              proj_a, proj_b, proj_c, gate_up, gate_b, layer, 1024, 512, COL_G_LOW)
        x = _matmul_residual(merged, w_out, layer, x.reshape(m, d), gate_m, 1024, 512, "out_proj").reshape(b, l, d)
        h = _norm_mod(x, norm_mlp[layer], scale_f, shift_f)
        act = _ffn_up_act(h.reshape(m, d), mlp_up, layer, mlp_conv_w[layer], mlp_conv_b[layer], l, 1024, 512)
        x = _matmul_residual(act, mlp_down, layer, x.reshape(m, d), gate_f, 512, 512, "ffn_down").reshape(b, l, d)
    return x
```
